```python
import math
import jax, jax.numpy as jnp
from jax import lax
import numpy as np

D_MODEL = 1024
BATCH = 32
SEQ = 2048
DEPTH = 1

N_HEADS = 8
HEAD_DIM = 64
ATT_WIDTH = N_HEADS * HEAD_DIM
PATTERNS = ((128, 1), (512, 4), (2048, 16))
SSM_GROUPS = 16
SSM_GROUP_CH = 16
SSM_WIDTH = SSM_GROUPS * SSM_GROUP_CH
SSM_STATE = 64
D_FF = 2048
CONV_W = 3
IN_WIDTH = 3 * ATT_WIDTH + SSM_WIDTH + 2 * D_MODEL
EPS = 1e-6
NEG_INF = -1e30

kernel_name = "gated_dilated_attn_s5_hybrid_block"


def _rmsnorm(x, g):
    x32 = x.astype(jnp.float32)
    y = x32 * lax.rsqrt(jnp.mean(x32 * x32, axis=-1, keepdims=True) + EPS)
    return y.astype(x.dtype) * g


def _modulate(x, g, shift, scale):
    return _rmsnorm(x, g) * (1 + scale[:, None, :]) + shift[:, None, :]


def _alibi_slopes():
    return np.array([2.0 ** (-8.0 * (h + 1) / N_HEADS) for h in range(N_HEADS)], dtype=np.float32)


def _dilated_window_attention(q, k, v, slopes, window, dilation):
    b, s, h, hd = q.shape
    w = window // dilation
    L = s // dilation
    nb = -(-L // w)
    Lp = nb * w
    X = b * dilation

    def to_sub(t):
        t = t.reshape(b, L, dilation, h, hd).transpose(0, 2, 3, 1, 4)
        return t.reshape(X, h, L, hd)

    qs, ks, vs = to_sub(q), to_sub(k), to_sub(v)
    qb = jnp.pad(qs, ((0, 0), (0, 0), (0, Lp - L), (0, 0))).reshape(X, h, nb, w, hd)
    kp = jnp.pad(ks, ((0, 0), (0, 0), (w, Lp - L), (0, 0)))
    vp = jnp.pad(vs, ((0, 0), (0, 0), (w, Lp - L), (0, 0)))
    kb = jnp.concatenate([kp[:, :, :Lp].reshape(X, h, nb, w, hd),
                          kp[:, :, w:].reshape(X, h, nb, w, hd)], axis=3)
    vb = jnp.concatenate([vp[:, :, :Lp].reshape(X, h, nb, w, hd),
                          vp[:, :, w:].reshape(X, h, nb, w, hd)], axis=3)

    a_idx = np.arange(w)[:, None]
    j_idx = np.arange(2 * w)[None, :]
    dist = (w + a_idx - j_idx).astype(np.float32)
    kpos = np.arange(nb)[:, None, None] * w - w + j_idx[None]
    valid = (dist[None] >= 0) & (dist[None] <= w) & (kpos >= 0)
    bias = -(slopes[:, None, None] * dilation) * dist[None]

    scale = HEAD_DIM ** -0.5
    sc = jnp.einsum('xhnqd,xhnkd->xhnqk', qb, kb).astype(jnp.float32) * scale
    sc = jnp.where(valid, sc + bias[:, None], NEG_INF)
    m = jnp.max(sc, axis=-1, keepdims=True)
    p = jnp.exp(sc - m)
    den = jnp.sum(p, axis=-1, keepdims=True)
    o = jnp.einsum('xhnqk,xhnkd->xhnqd', p, vb.astype(jnp.float32)) / den
    lse = (m + jnp.log(den))[..., 0]

    o = o.reshape(X, h, Lp, hd)[:, :, :L].reshape(b, dilation, h, L, hd)
    o = o.transpose(0, 3, 1, 2, 4).reshape(b, s, h, hd)
    lse = lse.reshape(X, h, Lp)[:, :, :L].reshape(b, dilation, h, L)
    lse = lse.transpose(0, 3, 1, 2).reshape(b, s, h)
    return o, lse


def _s5_branch(u, a_re, a_im, log_dt, b_re, b_im, c_re, c_im, d_skip, w_glu, b_glu):
    f32 = jnp.float32
    bsz, s, _ = u.shape
    lr, li = a_re.astype(f32), a_im.astype(f32)
    dt = jnp.exp(log_dt.astype(f32))[:, None]
    mag = jnp.exp(lr * dt)
    ang = li * dt
    ab_re, ab_im = mag * jnp.cos(ang), mag * jnp.sin(ang)
    nr, ni = ab_re - 1.0, ab_im
    den = lr * lr + li * li
    f_re = (nr * lr + ni * li) / den
    f_im = (ni * lr - nr * li) / den
    br, bi = b_re.astype(f32), b_im.astype(f32)
    bb_re = f_re[..., None] * br - f_im[..., None] * bi
    bb_im = f_re[..., None] * bi + f_im[..., None] * br

    ug = u.astype(f32).reshape(bsz, s, SSM_GROUPS, SSM_GROUP_CH)
    bu_re = jnp.einsum('bsgc,gnc->bsgn', ug, bb_re)
    bu_im = jnp.einsum('bsgc,gnc->bsgn', ug, bb_im)
    a_re_t = jnp.broadcast_to(ab_re, bu_re.shape)
    a_im_t = jnp.broadcast_to(ab_im, bu_re.shape)

    def combine(left, right):
        ar1, ai1, xr1, xi1 = left
        ar2, ai2, xr2, xi2 = right
        return (ar2 * ar1 - ai2 * ai1,
                ar2 * ai1 + ai2 * ar1,
                ar2 * xr1 - ai2 * xi1 + xr2,
                ar2 * xi1 + ai2 * xr1 + xi2)

    _, _, xr, xi = lax.associative_scan(combine, (a_re_t, a_im_t, bu_re, bu_im), axis=1)
    y = (jnp.einsum('bsgn,gcn->bsgc', xr, c_re.astype(f32))
         - jnp.einsum('bsgn,gcn->bsgc', xi, c_im.astype(f32))
         + d_skip.astype(f32).reshape(SSM_GROUPS, SSM_GROUP_CH) * ug)
    y = y.reshape(bsz, s, SSM_WIDTH).astype(u.dtype)
    y = jax.nn.gelu(y)
    return y * jax.nn.sigmoid(y @ w_glu + b_glu)


def _hybrid_mixer(u, w_in, b_gate, a_re, a_im, log_dt, b_re, b_im, c_re, c_im,
                  d_skip, w_glu, b_glu, w_proj_att, w_proj_ssm, w_out):
    bsz, s, _ = u.shape
    proj = u @ w_in
    q, k, v, us, g_att, g_ssm = jnp.split(
        proj, [ATT_WIDTH, 2 * ATT_WIDTH, 3 * ATT_WIDTH, 3 * ATT_WIDTH + SSM_WIDTH,
               3 * ATT_WIDTH + SSM_WIDTH + D_MODEL], axis=-1)
    q = q.reshape(bsz, s, N_HEADS, HEAD_DIM)
    k = k.reshape(bsz, s, N_HEADS, HEAD_DIM)
    v = v.reshape(bsz, s, N_HEADS, HEAD_DIM)

    slopes = _alibi_slopes()
    outs, lses = [], []
    for window, dilation in PATTERNS:
        o, lse = _dilated_window_attention(q, k, v, slopes, window, dilation)
        outs.append(o)
        lses.append(lse)
    wts = jax.nn.softmax(jnp.stack(lses, axis=0), axis=0)
    o_att = jnp.sum(wts[..., None] * jnp.stack(outs, axis=0), axis=0)
    o_att = o_att.reshape(bsz, s, ATT_WIDTH).astype(u.dtype)
    y_att = o_att @ w_proj_att

    y_ssm = _s5_branch(us, a_re, a_im, log_dt, b_re, b_im, c_re, c_im,
                       d_skip, w_glu, b_glu) @ w_proj_ssm

    gb_att, gb_ssm = jnp.split(b_gate, 2, axis=-1)
    merged = jax.nn.sigmoid(g_att + gb_att) * y_att + jax.nn.sigmoid(g_ssm + gb_ssm) * y_ssm
    return merged @ w_out


def _conv_ffn(u, w_up, w_conv, b_conv, w_down):
    s = u.shape[1]
    a, val = jnp.split(u @ w_up, 2, axis=-1)
    ap = jnp.pad(a, ((0, 0), (CONV_W - 1, 0), (0, 0)))
    conv = b_conv
    for j in range(CONV_W):
        conv = conv + w_conv[j] * ap[:, CONV_W - 1 - j:CONV_W - 1 - j + s]
    return (jax.nn.silu(conv) * val) @ w_down


def setup_inputs(seed: int = 0) -> dict:
    key = jax.random.key(seed)
    ks = jax.random.split(key, 32)
    f32 = jnp.float32
    L, D, G, N, C = DEPTH, D_MODEL, SSM_GROUPS, SSM_STATE, SSM_GROUP_CH
    nrm = lambda k, shape, sc: jax.random.normal(k, shape, f32) * sc
    inp = {}
    inp["x"] = nrm(ks[0], (BATCH, SEQ, D), 1.0)
    inp["c"] = nrm(ks[1], (BATCH, D), 1.0)
    inp["w_ada"] = nrm(ks[2], (L, D, 6 * D), 0.5 * D ** -0.5)
    inp["b_ada"] = nrm(ks[3], (L, 6 * D), 0.02)
    inp["g_mix"] = 1.0 + nrm(ks[4], (L, D), 0.02)
    inp["w_in"] = nrm(ks[5], (L, D, IN_WIDTH), D ** -0.5)
    inp["b_gate"] = nrm(ks[6], (L, 2 * D), 0.02)
    inp["a_re"] = -0.5 + nrm(ks[7], (L, G, N), 0.01)
    inp["a_im"] = jnp.pi * jnp.arange(N, dtype=f32)[None, None, :] + nrm(ks[8], (L, G, N), 0.01)
    inp["log_dt"] = jax.random.uniform(ks[9], (L, G), f32, math.log(1e-3), math.log(1e-1))
    inp["b_re"] = nrm(ks[10], (L, G, N, C), (2 * C) ** -0.5)
    inp["b_im"] = nrm(ks[11], (L, G, N, C), (2 * C) ** -0.5)
    inp["c_re"] = nrm(ks[12], (L, G, C, N), (2 * N) ** -0.5)
    inp["c_im"] = nrm(ks[13], (L, G, C, N), (2 * N) ** -0.5)
    inp["d_skip"] = nrm(ks[14], (L, SSM_WIDTH), 1.0)
    inp["w_glu"] = nrm(ks[15], (L, SSM_WIDTH, SSM_WIDTH), SSM_WIDTH ** -0.5)
    inp["b_glu"] = nrm(ks[16], (L, SSM_WIDTH), 0.02)
    inp["w_proj_att"] = nrm(ks[17], (L, ATT_WIDTH, D), ATT_WIDTH ** -0.5)
    inp["w_proj_ssm"] = nrm(ks[18], (L, SSM_WIDTH, D), SSM_WIDTH ** -0.5)
    inp["w_out"] = nrm(ks[19], (L, D, D), D ** -0.5)
    inp["g_ffn"] = 1.0 + nrm(ks[20], (L, D), 0.02)
    inp["w_up"] = nrm(ks[21], (L, D, 2 * D_FF), D ** -0.5)
    inp["w_conv"] = nrm(ks[22], (L, CONV_W, D_FF), CONV_W ** -0.5)
    inp["b_conv"] = nrm(ks[23], (L, D_FF), 0.02)
    inp["w_down"] = nrm(ks[24], (L, D_FF, D), D_FF ** -0.5)
    inp["g_final"] = 1.0 + nrm(ks[25], (D,), 0.02)
    return inp


def reference(x, c, w_ada, b_ada, g_mix, w_in, b_gate, a_re, a_im, log_dt, b_re, b_im,
              c_re, c_im, d_skip, w_glu, b_glu, w_proj_att, w_proj_ssm, w_out,
              g_ffn, w_up, w_conv, b_conv, w_down, g_final):
    h = x
    c_act = jax.nn.silu(c)
    for l in range(DEPTH):
        mod = c_act @ w_ada[l] + b_ada[l]
        sh1, sc1, gt1, sh2, sc2, gt2 = jnp.split(mod, 6, axis=-1)
        u = _modulate(h, g_mix[l], sh1, sc1)
        h = h + gt1[:, None, :] * _hybrid_mixer(
            u, w_in[l], b_gate[l], a_re[l], a_im[l], log_dt[l], b_re[l], b_im[l],
            c_re[l], c_im[l], d_skip[l], w_glu[l], b_glu[l],
            w_proj_att[l], w_proj_ssm[l], w_out[l])
        u = _modulate(h, g_ffn[l], sh2, sc2)
        h = h + gt2[:, None, :] * _conv_ffn(u, w_up[l], w_conv[l], b_conv[l], w_down[l])
    return _rmsnorm(h, g_final)
```

```python
import functools
import math

import jax
import jax.numpy as jnp
import numpy as np
from jax import lax
from jax.experimental import pallas as pl
from jax.experimental.pallas import tpu as pltpu

D_MODEL = 1024
N_HEADS = 8
HEAD_DIM = 64
ATT_WIDTH = N_HEADS * HEAD_DIM
PATTERNS = ((128, 1), (512, 4), (2048, 16))
SSM_GROUPS = 16
SSM_GROUP_CH = 16
SSM_WIDTH = SSM_GROUPS * SSM_GROUP_CH
SSM_STATE = 64
D_FF = 2048
CONV_W = 3
EPS = 1e-6
NEG_INF = -1e30

LANES = 128
SUBLANES = 8
VMEM_LIMIT_BYTES = 56 * 1024 * 1024

ATT_BLOCK = 128
SSM_CHUNK = 8
STATE_COLS = 2 * SSM_GROUPS * SSM_STATE

BF16 = jnp.bfloat16
F32 = jnp.float32


def _const_spec(shape):
    zeros = (0,) * len(shape)
    return pl.BlockSpec(shape, lambda *_: zeros, pipeline_mode=pl.Buffered(1))


def _params(n_axes):
    return pltpu.CompilerParams(
        dimension_semantics=("arbitrary",) * n_axes,
        vmem_limit_bytes=VMEM_LIMIT_BYTES)


def _ada_body(c_ref, w_ref, b_ref, o_ref):
    c = c_ref[...]
    act = (c * jax.nn.sigmoid(c)).astype(BF16)
    o_ref[...] = jnp.dot(act, w_ref[...], preferred_element_type=F32) + b_ref[...]


def _ada(c, w_ada, b_ada):
    bsz = c.shape[0]
    n_out = w_ada.shape[1]
    tn = 1536
    return pl.pallas_call(
        _ada_body,
        grid=(n_out // tn,),
        in_specs=[_const_spec((bsz, D_MODEL)),
                  pl.BlockSpec((D_MODEL, tn), lambda j: (0, j)),
                  pl.BlockSpec((1, tn), lambda j: (0, j))],
        out_specs=pl.BlockSpec((bsz, tn), lambda j: (0, j)),
        out_shape=jax.ShapeDtypeStruct((bsz, n_out), F32),
        compiler_params=_params(1),
        name="ada",
    )(c, w_ada.astype(BF16), b_ada.reshape(1, n_out))


def _rms_modulate(x, gain, shift, scale):
    ms = jnp.mean(x * x, axis=-1, keepdims=True)
    return (x * lax.rsqrt(ms + EPS)) * (gain * (1.0 + scale)) + shift


def _inproj_body(x_ref, mod_ref, g_ref, w_ref, bg_ref,
                 q_ref, k_ref, v_ref, us_ref, sa_ref, ss_ref):
    u = _rms_modulate(x_ref[...], g_ref[...], mod_ref[0:1, :], mod_ref[1:2, :]).astype(BF16)

    def proj(lo, hi):
        return jnp.dot(u, w_ref[:, lo:hi], preferred_element_type=F32)

    a = ATT_WIDTH
    q_ref[...] = (proj(0, a) * (HEAD_DIM ** -0.5)).astype(BF16)
    k_ref[...] = proj(a, 2 * a).astype(BF16)
    v_ref[...] = proj(2 * a, 3 * a).astype(BF16)
    o = 3 * a
    us_ref[...] = proj(o, o + SSM_WIDTH).astype(BF16)
    o += SSM_WIDTH
    sa_ref[...] = jax.nn.sigmoid(proj(o, o + D_MODEL) + bg_ref[:, 0:D_MODEL]).astype(BF16)
    o += D_MODEL
    ss_ref[...] = jax.nn.sigmoid(proj(o, o + D_MODEL) + bg_ref[:, D_MODEL:]).astype(BF16)


def _inproj(x, mod3, g_mix, w_in, b_gate, tm):
    bsz, seq, d = x.shape
    in_width = w_in.shape[1]

    def tok(width):
        return pl.BlockSpec((None, tm, width), lambda b, s: (b, s, 0))

    def out(width):
        return jax.ShapeDtypeStruct((bsz, seq, width), BF16)

    return pl.pallas_call(
        _inproj_body,
        grid=(bsz, seq // tm),
        in_specs=[tok(d),
                  pl.BlockSpec((None, 6, d), lambda b, s: (b, 0, 0)),
                  _const_spec((1, d)),
                  _const_spec((d, in_width)),
                  _const_spec((1, 2 * d))],
        out_specs=[tok(ATT_WIDTH), tok(ATT_WIDTH), tok(ATT_WIDTH), tok(SSM_WIDTH), tok(d), tok(d)],
        out_shape=[out(ATT_WIDTH), out(ATT_WIDTH), out(ATT_WIDTH), out(SSM_WIDTH), out(d), out(d)],
        compiler_params=_params(2),
        name="inproj",
    )(x, mod3, g_mix.reshape(1, d), w_in.astype(BF16), b_gate.reshape(1, 2 * d))


def _alibi_bias_table(dilation, kw):
    w = ATT_BLOCK
    a = np.arange(w)[:, None]
    j = np.arange(kw)[None, :]
    tabs = []
    for h in range(N_HEADS):
        slope = 2.0 ** (-8.0 * (h + 1) / N_HEADS)
        for sel in range(2):
            dist = (sel * w + a - j).astype(np.float32)
            valid = (dist >= 0) & (dist <= w)
            tabs.append(np.where(valid, -(slope * dilation) * dist, NEG_INF).astype(np.float32))
    return np.stack(tabs)


def _attn_body(q_ref, k_ref, v_ref, bias_ref, o_ref, lse_ref, *, sub_len, kw):
    w = ATT_BLOCK
    i = pl.program_id(2)
    start = pl.multiple_of(jnp.clip((i - 1) * w, 0, sub_len - kw), w)
    sel = jnp.minimum(i, 1)
    q = q_ref[...]
    kwin = k_ref[pl.ds(start, kw), :]
    vwin = v_ref[pl.ds(start, kw), :]
    lane = lax.broadcasted_iota(jnp.int32, (w, LANES), 1)
    low_half = lane < HEAD_DIM
    zero = jnp.zeros((), BF16)
    lse_acc = jnp.zeros((w, LANES), F32)
    for pair in range(N_HEADS // 2):
        cols = slice(pair * LANES, (pair + 1) * LANES)
        qp, kp, vp = q[:, cols], kwin[:, cols], vwin[:, cols]
        qs = jnp.concatenate([jnp.where(low_half, qp, zero), jnp.where(low_half, zero, qp)], axis=0)
        s2 = lax.dot_general(qs, kp, (((1,), (1,)), ((), ())), preferred_element_type=F32)
        outs = []
        for hh in range(2):
            h = 2 * pair + hh
            s = s2[hh * w:(hh + 1) * w] + bias_ref[2 * h + sel]
            m = jnp.max(s, axis=-1, keepdims=True)
            p = jnp.exp(s - m)
            den = jnp.sum(p, axis=-1, keepdims=True)
            o = jnp.dot(p.astype(BF16), vp, preferred_element_type=F32)
            outs.append(o * (1.0 / den))
            lse_acc = jnp.where(lane == h, m + jnp.log(den), lse_acc)
        o_ref[:, cols] = jnp.where(low_half, outs[0], outs[1]).astype(BF16)
    lse_ref[...] = lse_acc


def _attention_pattern(qs, ks, vs, dilation):
    bsz, d, sub_len, _ = qs.shape
    w = ATT_BLOCK
    kw = min(2 * w, sub_len)
    bias = jnp.asarray(_alibi_bias_table(dilation, kw))
    qspec = pl.BlockSpec((None, None, w, ATT_WIDTH), lambda b, r, i: (b, r, i, 0))
    kvspec = pl.BlockSpec((None, None, sub_len, ATT_WIDTH), lambda b, r, i: (b, r, 0, 0))
    return pl.pallas_call(
        functools.partial(_attn_body, sub_len=sub_len, kw=kw),
        grid=(bsz, d, sub_len // w),
        in_specs=[qspec, kvspec, kvspec, _const_spec(bias.shape)],
        out_specs=[qspec, pl.BlockSpec((None, None, w, LANES), lambda b, r, i: (b, r, i, 0))],
        out_shape=[jax.ShapeDtypeStruct((bsz, d, sub_len, ATT_WIDTH), BF16),
                   jax.ShapeDtypeStruct((bsz, d, sub_len, LANES), F32)],
        compiler_params=_params(3),
        name=f"attn_d{dilation}",
    )(qs, ks, vs, bias)


def _to_sub(t, dilation):
    b, s, c = t.shape
    return t.reshape(b, s // dilation, dilation, c).transpose(0, 2, 1, 3)


def _from_sub(t):
    b, d, sub_len, c = t.shape
    return t.transpose(0, 2, 1, 3).reshape(b, d * sub_len, c)


def _ssm_matrices(a_re, a_im, log_dt, b_re, b_im, c_re, c_im, d_skip):
    lc, g_n, n_st, ch = SSM_CHUNK, SSM_GROUPS, SSM_STATE, SSM_GROUP_CH
    hi = lax.Precision.HIGHEST
    lr, li = a_re.astype(F32), a_im.astype(F32)
    dt = jnp.exp(log_dt.astype(F32))[:, None]
    mag = jnp.exp(lr * dt)
    ang = li * dt
    ab_re, ab_im = mag * jnp.cos(ang), mag * jnp.sin(ang)
    nr, ni = ab_re - 1.0, ab_im
    den = lr * lr + li * li
    f_re = (nr * lr + ni * li) / den
    f_im = (ni * lr - nr * li) / den
    bb_re = f_re[..., None] * b_re - f_im[..., None] * b_im
    bb_im = f_re[..., None] * b_im + f_im[..., None] * b_re

    kk = jnp.arange(lc + 1, dtype=F32)[:, None, None]
    pw_mag = jnp.exp(kk * (lr * dt)[None])
    pw_re = pw_mag * jnp.cos(kk * ang[None])
    pw_im = pw_mag * jnp.sin(kk * ang[None])

    eye_g = jnp.eye(g_n, dtype=F32)

    cp_re = c_re[None] * pw_re[:, :, None, :] - c_im[None] * pw_im[:, :, None, :]
    cp_im = c_re[None] * pw_im[:, :, None, :] + c_im[None] * pw_re[:, :, None, :]

    taps = (jnp.einsum('kgcn,gni->kgci', cp_re[:lc], bb_re, precision=hi)
            - jnp.einsum('kgcn,gni->kgci', cp_im[:lc], bb_im, precision=hi))
    skip = d_skip.reshape(g_n, ch)
    taps = taps.at[0].add(skip[:, :, None] * jnp.eye(ch, dtype=F32)[None])
    kd = jnp.einsum('kgoi,gh->kgiho', taps, eye_g).reshape(lc, SSM_WIDTH, SSM_WIDTH)
    zero_blk = jnp.zeros((SSM_WIDTH, SSM_WIDTH), F32)
    toeplitz = jnp.concatenate(
        [jnp.concatenate([kd[j - s] if j >= s else zero_blk for j in range(lc)], axis=1)
         for s in range(lc)], axis=0)

    carry = jnp.stack([cp_re[1:], -cp_im[1:]], axis=0)
    carry = jnp.einsum('pjgcn,gh->pgnjhc', carry, eye_g).reshape(STATE_COLS, lc * SSM_WIDTH)
    wy = jnp.concatenate([carry, toeplitz], axis=0)

    rp_re, rp_im = pw_re[:lc][::-1], pw_im[:lc][::-1]
    bp_re = rp_re[..., None] * bb_re[None] - rp_im[..., None] * bb_im[None]
    bp_im = rp_re[..., None] * bb_im[None] + rp_im[..., None] * bb_re[None]
    bst = jnp.stack([bp_re, bp_im], axis=0)
    bst = jnp.einsum('psgnc,gh->sgcphn', bst, eye_g).reshape(lc * SSM_WIDTH, STATE_COLS)

    half = g_n * n_st
    a_pow = jnp.concatenate([pw_re[lc].reshape(1, half), pw_im[lc].reshape(1, half)], axis=1)
    return bst.astype(BF16), wy.astype(BF16), a_pow


def _ssm_body(u_ref, bst_ref, wy_ref, apow_ref, wglu_ref, bglu_ref, o_ref, st_ref,
              *, nb, n_chunks):
    lc = SSM_CHUNK
    half = STATE_COLS // 2
    rows = nb * n_chunks
    u2 = u_ref[...].reshape(rows, lc * SSM_WIDTH)
    st_ref[...] = jnp.dot(u2, bst_ref[...], preferred_element_type=F32)

    a_r = apow_ref[:, 0:half]
    a_i = apow_ref[:, half:]

    def step(c, carry):
        new = []
        for b in range(nb):
            x_r, x_i = carry[2 * b], carry[2 * b + 1]
            row = pl.ds(b * n_chunks + c, 1)
            s_r = st_ref[row, 0:half]
            s_i = st_ref[row, half:]
            st_ref[row, 0:half] = x_r
            st_ref[row, half:] = x_i
            new.append(a_r * x_r - a_i * x_i + s_r)
            new.append(a_r * x_i + a_i * x_r + s_i)
        return tuple(new)

    init = tuple(jnp.zeros((1, half), F32) for _ in range(2 * nb))
    lax.fori_loop(0, n_chunks, step, init)

    x_in = st_ref[...].astype(BF16)
    for j in range(lc):
        kdim = (j + 1) * SSM_WIDTH
        cols = slice(j * SSM_WIDTH, (j + 1) * SSM_WIDTH)
        y = (jnp.dot(x_in, wy_ref[0:STATE_COLS, cols], preferred_element_type=F32)
             + jnp.dot(u2[:, :kdim], wy_ref[STATE_COLS:STATE_COLS + kdim, cols],
                       preferred_element_type=F32))
        y = jax.nn.gelu(y)
        gate = jnp.dot(y.astype(BF16), wglu_ref[...], preferred_element_type=F32) + bglu_ref[...]
        o_ref[:, :, cols] = (y * jax.nn.sigmoid(gate)).astype(BF16).reshape(nb, n_chunks, SSM_WIDTH)


def _ssm(us, bst, wy, a_pow, w_glu, b_glu, nb):
    bsz, seq, _ = us.shape
    lc = SSM_CHUNK
    n_chunks = seq // lc
    u2 = us.reshape(bsz, n_chunks, lc * SSM_WIDTH)
    blk = pl.BlockSpec((nb, n_chunks, lc * SSM_WIDTH), lambda b: (b, 0, 0))
    out = pl.pallas_call(
        functools.partial(_ssm_body, nb=nb, n_chunks=n_chunks),
        grid=(bsz // nb,),
        in_specs=[blk, _const_spec(bst.shape), _const_spec(wy.shape), _const_spec(a_pow.shape),
                  _const_spec((SSM_WIDTH, SSM_WIDTH)), _const_spec((1, SSM_WIDTH))],
        out_specs=blk,
        out_shape=jax.ShapeDtypeStruct(u2.shape, BF16),
        scratch_shapes=[pltpu.VMEM((nb * n_chunks, STATE_COLS), F32)],
        compiler_params=_params(1),
        name="ssm",
    )(u2, bst, wy, a_pow, w_glu.astype(BF16), b_glu.reshape(1, SSM_WIDTH))
    return out.reshape(bsz, seq, SSM_WIDTH)


def _tail_body(x_ref, o1_ref, o2_ref, o3_ref, l1_ref, l2_ref, l3_ref, sg_ref, sa_ref, ss_ref,
               mod_ref, ex_ref, wpa_ref, wps_ref, wo_ref, gf_ref, wup_ref, wc_ref, bc_ref,
               wdn_ref, gfin_ref, out_ref, carry_ref):
    tm = x_ref.shape[0]

    @pl.when(pl.program_id(1) == 0)
    def _():
        carry_ref[...] = jnp.zeros_like(carry_ref)

    l1, l2, l3 = l1_ref[...], l2_ref[...], l3_ref[...]
    mx = jnp.maximum(jnp.maximum(l1, l2), l3)
    e1, e2, e3 = jnp.exp(l1 - mx), jnp.exp(l2 - mx), jnp.exp(l3 - mx)
    inv = 1.0 / (e1 + e2 + e3)
    o_att = jnp.zeros((tm, ATT_WIDTH), F32)
    for e, o_ref in ((e1, o1_ref), (e2, o2_ref), (e3, o3_ref)):
        wt = jnp.dot((e * inv).astype(BF16), ex_ref[...], preferred_element_type=F32)
        o_att = o_att + wt * o_ref[...].astype(F32)

    y_att = jnp.dot(o_att.astype(BF16), wpa_ref[...], preferred_element_type=F32)
    y_ssm = jnp.dot(sg_ref[...], wps_ref[...], preferred_element_type=F32)
    merged = sa_ref[...].astype(F32) * y_att + ss_ref[...].astype(F32) * y_ssm
    mix = jnp.dot(merged.astype(BF16), wo_ref[...], preferred_element_type=F32)
    h1 = x_ref[...] + mod_ref[2:3, :] * mix

    u = _rms_modulate(h1, gf_ref[...], mod_ref[3:4, :], mod_ref[4:5, :]).astype(BF16)
    a = jnp.dot(u, wup_ref[:, 0:D_FF], preferred_element_type=F32)
    val = jnp.dot(u, wup_ref[:, D_FF:], preferred_element_type=F32)

    row = lax.broadcasted_iota(jnp.int32, a.shape, 0)
    prev1 = carry_ref[SUBLANES - 1:SUBLANES, :]
    prev2 = carry_ref[SUBLANES - 2:SUBLANES - 1, :]
    a1 = jnp.where(row == 0, prev1, pltpu.roll(a, 1, 0))
    a2 = jnp.where(row == 0, prev2, jnp.where(row == 1, prev1, pltpu.roll(a, 2, 0)))
    carry_ref[...] = a[tm - SUBLANES:, :]
    conv = bc_ref[...] + wc_ref[0:1, :] * a + wc_ref[1:2, :] * a1 + wc_ref[2:3, :] * a2
    act = (conv * jax.nn.sigmoid(conv) * val).astype(BF16)
    ffn = jnp.dot(act, wdn_ref[...], preferred_element_type=F32)
    h2 = h1 + mod_ref[5:6, :] * ffn

    ms = jnp.mean(h2 * h2, axis=-1, keepdims=True)
    out_ref[...] = (h2 * lax.rsqrt(ms + EPS)) * gfin_ref[...]


def _head_expand_matrix():
    e = np.zeros((LANES, ATT_WIDTH), np.float32)
    for h in range(N_HEADS):
        e[h, h * HEAD_DIM:(h + 1) * HEAD_DIM] = 1.0
    return e


def _tail(x, o_pats, lse_pats, sglu, sig_att, sig_ssm, mod3, w_proj_att, w_proj_ssm, w_out,
          g_ffn, w_up, w_conv, b_conv, w_down, g_final, tm):
    bsz, seq, d = x.shape

    def tok(width):
        return pl.BlockSpec((None, tm, width), lambda b, s: (b, s, 0))

    expand = jnp.asarray(_head_expand_matrix(), BF16)
    in_specs = ([tok(d)] + [tok(ATT_WIDTH)] * 3 + [tok(LANES)] * 3
                + [tok(SSM_WIDTH), tok(d), tok(d),
                   pl.BlockSpec((None, 6, d), lambda b, s: (b, 0, 0)),
                   _const_spec(expand.shape),
                   _const_spec((ATT_WIDTH, d)), _const_spec((SSM_WIDTH, d)), _const_spec((d, d)),
                   _const_spec((1, d)), _const_spec((d, 2 * D_FF)), _const_spec((CONV_W, D_FF)),
                   _const_spec((1, D_FF)), _const_spec((D_FF, d)), _const_spec((1, d))])
    return pl.pallas_call(
        _tail_body,
        grid=(bsz, seq // tm),
        in_specs=in_specs,
        out_specs=tok(d),
        out_shape=jax.ShapeDtypeStruct((bsz, seq, d), F32),
        scratch_shapes=[pltpu.VMEM((SUBLANES, D_FF), F32)],
        compiler_params=_params(2),
        name="tail",
    )(x, *o_pats, *lse_pats, sglu, sig_att, sig_ssm, mod3, expand,
      w_proj_att.astype(BF16), w_proj_ssm.astype(BF16), w_out.astype(BF16),
      g_ffn.reshape(1, d), w_up.astype(BF16), w_conv, b_conv.reshape(1, D_FF),
      w_down.astype(BF16), g_final.reshape(1, d))


def kernel(x, c, w_ada, b_ada, g_mix, w_in, b_gate, a_re, a_im, log_dt, b_re, b_im, c_re, c_im,
           d_skip, w_glu, b_glu, w_proj_att, w_proj_ssm, w_out, g_ffn, w_up, w_conv, b_conv,
           w_down, g_final):
    depth = w_ada.shape[0]
    assert depth == 1, "the final RMSNorm is fused into the single layer's tail kernel"
    bsz, seq, d = x.shape
    l = 0
    mod3 = _ada(c, w_ada[l], b_ada[l]).reshape(bsz, 6, d)
    q, k, v, us, sig_att, sig_ssm = _inproj(x, mod3, g_mix[l], w_in[l], b_gate[l], tm=512)

    o_pats, lse_pats = [], []
    for _, dilation in PATTERNS:
        o_p, lse_p = _attention_pattern(_to_sub(q, dilation), _to_sub(k, dilation),
                                        _to_sub(v, dilation), dilation)
        o_pats.append(_from_sub(o_p))
        lse_pats.append(_from_sub(lse_p))

    bst, wy, a_pow = _ssm_matrices(a_re[l], a_im[l], log_dt[l], b_re[l], b_im[l], c_re[l],
                                   c_im[l], d_skip[l])
    sglu = _ssm(us, bst, wy, a_pow, w_glu[l], b_glu[l], nb=2)

    return _tail(x, o_pats, lse_pats, sglu, sig_att, sig_ssm, mod3, w_proj_att[l], w_proj_ssm[l],
                 w_out[l], g_ffn[l], w_up[l], w_conv[l], b_conv[l], w_down[l], g_final, tm=512)
```

```python
import functools
import math

import jax
import jax.numpy as jnp
import numpy as np
from jax import lax
from jax.experimental import pallas as pl
from jax.experimental.pallas import tpu as pltpu

D_MODEL = 1024
N_HEADS = 8
HEAD_DIM = 64
ATT_WIDTH = N_HEADS * HEAD_DIM
PATTERNS = ((128, 1), (512, 4), (2048, 16))
SSM_GROUPS = 16
SSM_GROUP_CH = 16
SSM_WIDTH = SSM_GROUPS * SSM_GROUP_CH
SSM_STATE = 64
D_FF = 2048
CONV_W = 3
EPS = 1e-6
NEG_INF = -1e30

LANES = 128
SUBLANES = 8
VMEM_LIMIT_BYTES = 56 * 1024 * 1024

ATT_BLOCK = 128
ATT_GROUP = 4
SSM_CHUNK = 8
STATE_COLS = 2 * SSM_GROUPS * SSM_STATE

BF16 = jnp.bfloat16
F32 = jnp.float32


def _const_spec(shape):
    zeros = (0,) * len(shape)
    return pl.BlockSpec(shape, lambda *_: zeros, pipeline_mode=pl.Buffered(1))


def _params(n_axes):
    return pltpu.CompilerParams(
        dimension_semantics=("arbitrary",) * n_axes,
        vmem_limit_bytes=VMEM_LIMIT_BYTES)


def _ada_body(c_ref, w_ref, b_ref, o_ref):
    c = c_ref[...]
    act = (c * jax.nn.sigmoid(c)).astype(BF16)
    o_ref[...] = jnp.dot(act, w_ref[...], preferred_element_type=F32) + b_ref[...]


def _ada(c, w_ada, b_ada):
    bsz = c.shape[0]
    n_out = w_ada.shape[1]
    tn = 1536
    return pl.pallas_call(
        _ada_body,
        grid=(n_out // tn,),
        in_specs=[_const_spec((bsz, D_MODEL)),
                  pl.BlockSpec((D_MODEL, tn), lambda j: (0, j)),
                  pl.BlockSpec((1, tn), lambda j: (0, j))],
        out_specs=pl.BlockSpec((bsz, tn), lambda j: (0, j)),
        out_shape=jax.ShapeDtypeStruct((bsz, n_out), F32),
        compiler_params=_params(1),
        name="ada",
    )(c, w_ada.astype(BF16), b_ada.reshape(1, n_out))


def _rms_modulate(x, gain, shift, scale):
    ms = jnp.mean(x * x, axis=-1, keepdims=True)
    return (x * lax.rsqrt(ms + EPS)) * (gain * (1.0 + scale)) + shift


def _inproj_body(x_ref, mod_ref, g_ref, w_ref, bg_ref,
                 q_ref, k_ref, v_ref, us_ref, sa_ref, ss_ref):
    u = _rms_modulate(x_ref[...], g_ref[...], mod_ref[0:1, :], mod_ref[1:2, :]).astype(BF16)

    def proj(lo, hi):
        return jnp.dot(u, w_ref[:, lo:hi], preferred_element_type=F32)

    a = ATT_WIDTH
    q_ref[...] = (proj(0, a) * (HEAD_DIM ** -0.5)).astype(BF16)
    k_ref[...] = proj(a, 2 * a).astype(BF16)
    v_ref[...] = proj(2 * a, 3 * a).astype(BF16)
    o = 3 * a
    us_ref[...] = proj(o, o + SSM_WIDTH).astype(BF16)
    o += SSM_WIDTH
    sa_ref[...] = jax.nn.sigmoid(proj(o, o + D_MODEL) + bg_ref[:, 0:D_MODEL]).astype(BF16)
    o += D_MODEL
    ss_ref[...] = jax.nn.sigmoid(proj(o, o + D_MODEL) + bg_ref[:, D_MODEL:]).astype(BF16)


def _inproj(x, mod3, g_mix, w_in, b_gate, tm):
    bsz, seq, d = x.shape
    in_width = w_in.shape[1]

    def tok(width):
        return pl.BlockSpec((None, tm, width), lambda b, s: (b, s, 0))

    def out(width):
        return jax.ShapeDtypeStruct((bsz, seq, width), BF16)

    return pl.pallas_call(
        _inproj_body,
        grid=(bsz, seq // tm),
        in_specs=[tok(d),
                  pl.BlockSpec((None, 6, d), lambda b, s: (b, 0, 0)),
                  _const_spec((1, d)),
                  _const_spec((d, in_width)),
                  _const_spec((1, 2 * d))],
        out_specs=[tok(ATT_WIDTH), tok(ATT_WIDTH), tok(ATT_WIDTH), tok(SSM_WIDTH), tok(d), tok(d)],
        out_shape=[out(ATT_WIDTH), out(ATT_WIDTH), out(ATT_WIDTH), out(SSM_WIDTH), out(d), out(d)],
        compiler_params=_params(2),
        name="inproj",
    )(x, mod3, g_mix.reshape(1, d), w_in.astype(BF16), b_gate.reshape(1, 2 * d))


def _alibi_bias_table(dilation, kw):
    w = ATT_BLOCK
    a = np.arange(w)[:, None]
    j = np.arange(kw)[None, :]
    tabs = []
    for h in range(N_HEADS):
        slope = 2.0 ** (-8.0 * (h + 1) / N_HEADS)
        for sel in range(2):
            dist = (sel * w + a - j).astype(np.float32)
            valid = (dist >= 0) & (dist <= w)
            tabs.append(np.where(valid, -(slope * dilation) * dist, NEG_INF).astype(np.float32))
    return np.stack(tabs)


def _pair_block(qp, kp, vp, bias0, bias1, low_half):
    w = ATT_BLOCK
    zero = jnp.zeros((), BF16)
    qs = jnp.concatenate([jnp.where(low_half, qp, zero), jnp.where(low_half, zero, qp)], axis=0)
    s2 = lax.dot_general(qs, kp, (((1,), (1,)), ((), ())), preferred_element_type=F32)
    outs, lses = [], []
    for hh, bias in enumerate((bias0, bias1)):
        s = s2[hh * w:(hh + 1) * w] + bias
        m = jnp.max(s, axis=-1, keepdims=True)
        p = jnp.exp(s - m)
        den = jnp.sum(p, axis=-1, keepdims=True)
        o = jnp.dot(p.astype(BF16), vp, preferred_element_type=F32)
        outs.append(o * (1.0 / den))
        lses.append(m + jnp.log(den))
    return jnp.where(low_half, outs[0], outs[1]), jnp.where(low_half, lses[0], lses[1])


def _attn_body(q_ref, k_ref, v_ref, b1_ref, b4_ref, b16_ref, o_ref,
               f32_ref, sub4f_ref, sub4_ref, sub16_ref, oacc_ref, lacc_ref, *, seq):
    w = ATT_BLOCK
    group = ATT_GROUP
    pair = pl.program_id(1)
    low_half = lax.broadcasted_iota(jnp.int32, (w, LANES), 1) < HEAD_DIM
    srcs = (q_ref, k_ref, v_ref)
    n4, n16 = seq // 4, seq // 16
    for t in range(3):
        f32_ref[t] = srcs[t][...].astype(F32)
    for t in range(3):
        for r4 in range(4):
            sub4f_ref[t, r4 * n4:(r4 + 1) * n4, :] = f32_ref[t, pl.ds(r4, n4, stride=4), :]
    for t in range(3):
        sub4_ref[t] = sub4f_ref[t].astype(BF16)
        for r4 in range(4):
            for q4 in range(4):
                r16 = 4 * q4 + r4
                sub16_ref[t, r16 * n16:(r16 + 1) * n16, :] = (
                    sub4f_ref[t, pl.ds(r4 * n4 + q4, n16, stride=4), :].astype(BF16))

    def run_pattern(pidx, dilation, bias_ref, load):
        sub_len = seq // dilation
        blocks_per_class = sub_len // w
        table_kw = bias_ref.shape[2]

        def one_block(r, i):
            base = r * sub_len
            qrow = pl.multiple_of(base + i * w, w)
            if isinstance(i, int) and i == 0:
                kw, krow, sel = w, pl.multiple_of(base, w), 0
            elif isinstance(i, int):
                kw, krow, sel = 2 * w, pl.multiple_of(base + (i - 1) * w, w), 1
            else:
                kw = 2 * w
                krow = pl.multiple_of(base + jnp.clip((i - 1) * w, 0, sub_len - kw), w)
                sel = jnp.minimum(i, 1)
            biases = [bias_ref[4 * pair + 2 * hh + sel] if kw == table_kw
                      else bias_ref[4 * pair + 2 * hh + sel, :, 0:kw] for hh in range(2)]
            o, lse = _pair_block(load(0, qrow, w), load(1, krow, kw), load(2, krow, kw),
                                 biases[0], biases[1], low_half)
            dst = pl.ds(dilation * w * i + r, w, stride=dilation) if dilation > 1 else pl.ds(qrow, w)
            oacc_ref[pidx, dst, :] = o
            lacc_ref[pidx, dst, :] = lse

        def body(g, carry):
            for j in range(group):
                if blocks_per_class == 1:
                    one_block(g * group + j, 0)
                elif blocks_per_class == group:
                    one_block(g, j)
                else:
                    one_block(0, g * group + j)
            return carry

        lax.fori_loop(0, seq // (w * group), body, 0)

    run_pattern(0, 1, b1_ref, lambda t, row, size: srcs[t][pl.ds(row, size), :])
    run_pattern(1, 4, b4_ref, lambda t, row, size: sub4_ref[t, pl.ds(row, size), :])
    run_pattern(2, 16, b16_ref, lambda t, row, size: sub16_ref[t, pl.ds(row, size), :])

    def merge(n, carry):
        rows = pl.ds(pl.multiple_of(n * w, w), w)
        l1, l2, l3 = lacc_ref[0, rows, :], lacc_ref[1, rows, :], lacc_ref[2, rows, :]
        mx = jnp.maximum(jnp.maximum(l1, l2), l3)
        e1, e2, e3 = jnp.exp(l1 - mx), jnp.exp(l2 - mx), jnp.exp(l3 - mx)
        num = e1 * oacc_ref[0, rows, :] + e2 * oacc_ref[1, rows, :] + e3 * oacc_ref[2, rows, :]
        o_ref[rows, :] = (num * (1.0 / (e1 + e2 + e3))).astype(BF16)
        return carry

    lax.fori_loop(0, seq // w, merge, 0)


def _attention(q, k, v):
    bsz, seq, _ = q.shape
    w = ATT_BLOCK
    biases = [jnp.asarray(_alibi_bias_table(d, min(2 * w, seq // d))) for _, d in PATTERNS]
    blk = pl.BlockSpec((None, seq, LANES), lambda b, p: (b, 0, p))
    return pl.pallas_call(
        functools.partial(_attn_body, seq=seq),
        grid=(bsz, N_HEADS // 2),
        in_specs=[blk, blk, blk] + [_const_spec(t.shape) for t in biases],
        out_specs=blk,
        out_shape=jax.ShapeDtypeStruct((bsz, seq, ATT_WIDTH), BF16),
        scratch_shapes=[pltpu.VMEM((3, seq, LANES), F32),
                        pltpu.VMEM((3, seq, LANES), F32),
                        pltpu.VMEM((3, seq, LANES), BF16),
                        pltpu.VMEM((3, seq, LANES), BF16),
                        pltpu.VMEM((3, seq, LANES), F32),
                        pltpu.VMEM((3, seq, LANES), F32)],
        compiler_params=_params(2),
        name="attn",
    )(q, k, v, *biases)


def _ssm_matrices(a_re, a_im, log_dt, b_re, b_im, c_re, c_im, d_skip):
    lc, g_n, n_st, ch = SSM_CHUNK, SSM_GROUPS, SSM_STATE, SSM_GROUP_CH
    hi = lax.Precision.HIGHEST
    lr, li = a_re.astype(F32), a_im.astype(F32)
    dt = jnp.exp(log_dt.astype(F32))[:, None]
    mag = jnp.exp(lr * dt)
    ang = li * dt
    ab_re, ab_im = mag * jnp.cos(ang), mag * jnp.sin(ang)
    nr, ni = ab_re - 1.0, ab_im
    den = lr * lr + li * li
    f_re = (nr * lr + ni * li) / den
    f_im = (ni * lr - nr * li) / den
    bb_re = f_re[..., None] * b_re - f_im[..., None] * b_im
    bb_im = f_re[..., None] * b_im + f_im[..., None] * b_re

    kk = jnp.arange(lc + 1, dtype=F32)[:, None, None]
    pw_mag = jnp.exp(kk * (lr * dt)[None])
    pw_re = pw_mag * jnp.cos(kk * ang[None])
    pw_im = pw_mag * jnp.sin(kk * ang[None])

    eye_g = jnp.eye(g_n, dtype=F32)

    cp_re = c_re[None] * pw_re[:, :, None, :] - c_im[None] * pw_im[:, :, None, :]
    cp_im = c_re[None] * pw_im[:, :, None, :] + c_im[None] * pw_re[:, :, None, :]

    taps = (jnp.einsum('kgcn,gni->kgci', cp_re[:lc], bb_re, precision=hi)
            - jnp.einsum('kgcn,gni->kgci', cp_im[:lc], bb_im, precision=hi))
    skip = d_skip.reshape(g_n, ch)
    taps = taps.at[0].add(skip[:, :, None] * jnp.eye(ch, dtype=F32)[None])
    kd = jnp.einsum('kgoi,gh->kgiho', taps, eye_g).reshape(lc, SSM_WIDTH, SSM_WIDTH)
    zero_blk = jnp.zeros((SSM_WIDTH, SSM_WIDTH), F32)
    toeplitz = jnp.concatenate(
        [jnp.concatenate([kd[j - s] if j >= s else zero_blk for j in range(lc)], axis=1)
         for s in range(lc)], axis=0)

    carry = jnp.stack([cp_re[1:], -cp_im[1:]], axis=0)
    carry = jnp.einsum('pjgcn,gh->pgnjhc', carry, eye_g).reshape(STATE_COLS, lc * SSM_WIDTH)
    wy = jnp.concatenate([carry, toeplitz], axis=0)

    rp_re, rp_im = pw_re[:lc][::-1], pw_im[:lc][::-1]
    bp_re = rp_re[..., None] * bb_re[None] - rp_im[..., None] * bb_im[None]
    bp_im = rp_re[..., None] * bb_im[None] + rp_im[..., None] * bb_re[None]
    bst = jnp.stack([bp_re, bp_im], axis=0)
    bst = jnp.einsum('psgnc,gh->sgcphn', bst, eye_g).reshape(lc * SSM_WIDTH, STATE_COLS)

    half = g_n * n_st
    a_pow = jnp.concatenate([pw_re[lc].reshape(1, half), pw_im[lc].reshape(1, half)], axis=1)
    return bst.astype(BF16), wy.astype(BF16), a_pow


def _ssm_body(u_ref, bst_ref, wy_ref, apow_ref, wglu_ref, bglu_ref, o_ref, st_ref,
              *, nb, n_chunks):
    lc = SSM_CHUNK
    half = STATE_COLS // 2
    rows = nb * n_chunks
    u2 = u_ref[...].reshape(rows, lc * SSM_WIDTH)
    st_ref[...] = jnp.dot(u2, bst_ref[...], preferred_element_type=F32)

    a_r = apow_ref[:, 0:half]
    a_i = apow_ref[:, half:]

    def step(c, carry):
        new = []
        for b in range(nb):
            x_r, x_i = carry[2 * b], carry[2 * b + 1]
            row = pl.ds(b * n_chunks + c, 1)
            s_r = st_ref[row, 0:half]
            s_i = st_ref[row, half:]
            st_ref[row, 0:half] = x_r
            st_ref[row, half:] = x_i
            new.append(a_r * x_r - a_i * x_i + s_r)
            new.append(a_r * x_i + a_i * x_r + s_i)
        return tuple(new)

    init = tuple(jnp.zeros((1, half), F32) for _ in range(2 * nb))
    lax.fori_loop(0, n_chunks, step, init)

    x_in = st_ref[...].astype(BF16)
    for j in range(lc):
        kdim = (j + 1) * SSM_WIDTH
        cols = slice(j * SSM_WIDTH, (j + 1) * SSM_WIDTH)
        y = (jnp.dot(x_in, wy_ref[0:STATE_COLS, cols], preferred_element_type=F32)
             + jnp.dot(u2[:, :kdim], wy_ref[STATE_COLS:STATE_COLS + kdim, cols],
                       preferred_element_type=F32))
        y = jax.nn.gelu(y)
        gate = jnp.dot(y.astype(BF16), wglu_ref[...], preferred_element_type=F32) + bglu_ref[...]
        o_ref[:, :, cols] = (y * jax.nn.sigmoid(gate)).astype(BF16).reshape(nb, n_chunks, SSM_WIDTH)


def _ssm(us, bst, wy, a_pow, w_glu, b_glu, nb):
    bsz, seq, _ = us.shape
    lc = SSM_CHUNK
    n_chunks = seq // lc
    u2 = us.reshape(bsz, n_chunks, lc * SSM_WIDTH)
    blk = pl.BlockSpec((nb, n_chunks, lc * SSM_WIDTH), lambda b: (b, 0, 0))
    out = pl.pallas_call(
        functools.partial(_ssm_body, nb=nb, n_chunks=n_chunks),
        grid=(bsz // nb,),
        in_specs=[blk, _const_spec(bst.shape), _const_spec(wy.shape), _const_spec(a_pow.shape),
                  _const_spec((SSM_WIDTH, SSM_WIDTH)), _const_spec((1, SSM_WIDTH))],
        out_specs=blk,
        out_shape=jax.ShapeDtypeStruct(u2.shape, BF16),
        scratch_shapes=[pltpu.VMEM((nb * n_chunks, STATE_COLS), F32)],
        compiler_params=_params(1),
        name="ssm",
    )(u2, bst, wy, a_pow, w_glu.astype(BF16), b_glu.reshape(1, SSM_WIDTH))
    return out.reshape(bsz, seq, SSM_WIDTH)


def _tail_body(x_ref, oatt_ref, sg_ref, sa_ref, ss_ref,
               mod_ref, wpa_ref, wps_ref, wo_ref, gf_ref, wup_ref, wc_ref, bc_ref,
               wdn_ref, gfin_ref, out_ref, carry_ref):
    tm = x_ref.shape[0]

    @pl.when(pl.program_id(1) == 0)
    def _():
        carry_ref[...] = jnp.zeros_like(carry_ref)

    y_att = jnp.dot(oatt_ref[...], wpa_ref[...], preferred_element_type=F32)
    y_ssm = jnp.dot(sg_ref[...], wps_ref[...], preferred_element_type=F32)
    merged = sa_ref[...].astype(F32) * y_att + ss_ref[...].astype(F32) * y_ssm
    mix = jnp.dot(merged.astype(BF16), wo_ref[...], preferred_element_type=F32)
    h1 = x_ref[...] + mod_ref[2:3, :] * mix

    u = _rms_modulate(h1, gf_ref[...], mod_ref[3:4, :], mod_ref[4:5, :]).astype(BF16)
    a = jnp.dot(u, wup_ref[:, 0:D_FF], preferred_element_type=F32)
    val = jnp.dot(u, wup_ref[:, D_FF:], preferred_element_type=F32)

    row = lax.broadcasted_iota(jnp.int32, a.shape, 0)
    prev1 = carry_ref[SUBLANES - 1:SUBLANES, :]
    prev2 = carry_ref[SUBLANES - 2:SUBLANES - 1, :]
    a1 = jnp.where(row == 0, prev1, pltpu.roll(a, 1, 0))
    a2 = jnp.where(row == 0, prev2, jnp.where(row == 1, prev1, pltpu.roll(a, 2, 0)))
    carry_ref[...] = a[tm - SUBLANES:, :]
    conv = bc_ref[...] + wc_ref[0:1, :] * a + wc_ref[1:2, :] * a1 + wc_ref[2:3, :] * a2
    act = (conv * jax.nn.sigmoid(conv) * val).astype(BF16)
    ffn = jnp.dot(act, wdn_ref[...], preferred_element_type=F32)
    h2 = h1 + mod_ref[5:6, :] * ffn

    ms = jnp.mean(h2 * h2, axis=-1, keepdims=True)
    out_ref[...] = (h2 * lax.rsqrt(ms + EPS)) * gfin_ref[...]


def _tail(x, o_att, sglu, sig_att, sig_ssm, mod3, w_proj_att, w_proj_ssm, w_out,
          g_ffn, w_up, w_conv, b_conv, w_down, g_final, tm):
    bsz, seq, d = x.shape

    def tok(width):
        return pl.BlockSpec((None, tm, width), lambda b, s: (b, s, 0))

    in_specs = [tok(d), tok(ATT_WIDTH), tok(SSM_WIDTH), tok(d), tok(d),
                pl.BlockSpec((None, 6, d), lambda b, s: (b, 0, 0)),
                _const_spec((ATT_WIDTH, d)), _const_spec((SSM_WIDTH, d)), _const_spec((d, d)),
                _const_spec((1, d)), _const_spec((d, 2 * D_FF)), _const_spec((CONV_W, D_FF)),
                _const_spec((1, D_FF)), _const_spec((D_FF, d)), _const_spec((1, d))]
    return pl.pallas_call(
        _tail_body,
        grid=(bsz, seq // tm),
        in_specs=in_specs,
        out_specs=tok(d),
        out_shape=jax.ShapeDtypeStruct((bsz, seq, d), F32),
        scratch_shapes=[pltpu.VMEM((SUBLANES, D_FF), F32)],
        compiler_params=_params(2),
        name="tail",
    )(x, o_att, sglu, sig_att, sig_ssm, mod3,
      w_proj_att.astype(BF16), w_proj_ssm.astype(BF16), w_out.astype(BF16),
      g_ffn.reshape(1, d), w_up.astype(BF16), w_conv, b_conv.reshape(1, D_FF),
      w_down.astype(BF16), g_final.reshape(1, d))


def kernel(x, c, w_ada, b_ada, g_mix, w_in, b_gate, a_re, a_im, log_dt, b_re, b_im, c_re, c_im,
           d_skip, w_glu, b_glu, w_proj_att, w_proj_ssm, w_out, g_ffn, w_up, w_conv, b_conv,
           w_down, g_final):
    depth = w_ada.shape[0]
    assert depth == 1, "the final RMSNorm is fused into the single layer's tail kernel"
    bsz, seq, d = x.shape
    l = 0
    mod3 = _ada(c, w_ada[l], b_ada[l]).reshape(bsz, 6, d)
    q, k, v, us, sig_att, sig_ssm = _inproj(x, mod3, g_mix[l], w_in[l], b_gate[l], tm=512)

    o_att = _attention(q, k, v)

    bst, wy, a_pow = _ssm_matrices(a_re[l], a_im[l], log_dt[l], b_re[l], b_im[l], c_re[l],
                                   c_im[l], d_skip[l])
    sglu = _ssm(us, bst, wy, a_pow, w_glu[l], b_glu[l], nb=2)

    return _tail(x, o_att, sglu, sig_att, sig_ssm, mod3, w_proj_att[l], w_proj_ssm[l],
                 w_out[l], g_ffn[l], w_up[l], w_conv[l], b_conv[l], w_down[l], g_final, tm=512)
```

```python
import functools
import math

import jax
import jax.numpy as jnp
import numpy as np
from jax import lax
from jax.experimental import pallas as pl
from jax.experimental.pallas import tpu as pltpu

D_MODEL = 1024
N_HEADS = 8
HEAD_DIM = 64
ATT_WIDTH = N_HEADS * HEAD_DIM
PATTERNS = ((128, 1), (512, 4), (2048, 16))
SSM_GROUPS = 16
SSM_GROUP_CH = 16
SSM_WIDTH = SSM_GROUPS * SSM_GROUP_CH
SSM_STATE = 64
D_FF = 2048
CONV_W = 3
EPS = 1e-6
NEG_INF = -1e30
LOG2E = math.log2(math.e)

LANES = 128
SUBLANES = 8
VMEM_LIMIT_BYTES = 56 * 1024 * 1024

ATT_BLOCK = 128
ATT_GROUP = 16
SSM_CHUNK = 8
STATE_COLS = 2 * SSM_GROUPS * SSM_STATE

BF16 = jnp.bfloat16
F32 = jnp.float32


def _const_spec(shape):
    zeros = (0,) * len(shape)
    return pl.BlockSpec(shape, lambda *_: zeros, pipeline_mode=pl.Buffered(1))


def _params(n_axes):
    return pltpu.CompilerParams(
        dimension_semantics=("arbitrary",) * n_axes,
        vmem_limit_bytes=VMEM_LIMIT_BYTES)


def _ada_body(c_ref, w_ref, b_ref, o_ref):
    c = c_ref[...]
    act = (c * jax.nn.sigmoid(c)).astype(BF16)
    o_ref[...] = jnp.dot(act, w_ref[...], preferred_element_type=F32) + b_ref[...]


def _ada(c, w_ada, b_ada):
    bsz = c.shape[0]
    n_out = w_ada.shape[1]
    tn = 1536
    return pl.pallas_call(
        _ada_body,
        grid=(n_out // tn,),
        in_specs=[_const_spec((bsz, D_MODEL)),
                  pl.BlockSpec((D_MODEL, tn), lambda j: (0, j)),
                  pl.BlockSpec((1, tn), lambda j: (0, j))],
        out_specs=pl.BlockSpec((bsz, tn), lambda j: (0, j)),
        out_shape=jax.ShapeDtypeStruct((bsz, n_out), F32),
        compiler_params=_params(1),
        name="ada",
    )(c, w_ada.astype(BF16), b_ada.reshape(1, n_out))


def _rms_modulate(x, gain, shift, scale):
    ms = jnp.mean(x * x, axis=-1, keepdims=True)
    return (x * lax.rsqrt(ms + EPS)) * (gain * (1.0 + scale)) + shift


def _inproj_body(x_ref, mod_ref, g_ref, w_ref, bg_ref,
                 q_ref, k_ref, v_ref, us_ref, sa_ref, ss_ref):
    u = _rms_modulate(x_ref[...], g_ref[...], mod_ref[0:1, :], mod_ref[1:2, :]).astype(BF16)

    def proj(lo, hi):
        return jnp.dot(u, w_ref[:, lo:hi], preferred_element_type=F32)

    a = ATT_WIDTH
    q_ref[...] = (proj(0, a) * (HEAD_DIM ** -0.5)).astype(BF16)
    k_ref[...] = proj(a, 2 * a).astype(BF16)
    v_ref[...] = proj(2 * a, 3 * a).astype(BF16)
    o = 3 * a
    us_ref[...] = proj(o, o + SSM_WIDTH).astype(BF16)
    o += SSM_WIDTH
    sa_ref[...] = jax.nn.sigmoid(proj(o, o + D_MODEL) + bg_ref[:, 0:D_MODEL]).astype(BF16)
    o += D_MODEL
    ss_ref[...] = jax.nn.sigmoid(proj(o, o + D_MODEL) + bg_ref[:, D_MODEL:]).astype(BF16)


def _inproj(x, mod3, g_mix, w_in, b_gate, tm):
    bsz, seq, d = x.shape
    in_width = w_in.shape[1]

    def tok(width):
        return pl.BlockSpec((None, tm, width), lambda b, s: (b, s, 0))

    def out(width):
        return jax.ShapeDtypeStruct((bsz, seq, width), BF16)

    return pl.pallas_call(
        _inproj_body,
        grid=(bsz, seq // tm),
        in_specs=[tok(d),
                  pl.BlockSpec((None, 6, d), lambda b, s: (b, 0, 0)),
                  _const_spec((1, d)),
                  _const_spec((d, in_width)),
                  _const_spec((1, 2 * d))],
        out_specs=[tok(ATT_WIDTH), tok(ATT_WIDTH), tok(ATT_WIDTH), tok(SSM_WIDTH), tok(d), tok(d)],
        out_shape=[out(ATT_WIDTH), out(ATT_WIDTH), out(ATT_WIDTH), out(SSM_WIDTH), out(d), out(d)],
        compiler_params=_params(2),
        name="inproj",
    )(x, mod3, g_mix.reshape(1, d), w_in.astype(BF16), b_gate.reshape(1, 2 * d))


def _alibi_distance_table(dilation, kw):
    w = ATT_BLOCK
    j = np.arange(kw)[:, None]
    a = np.arange(w)[None, :]
    tabs = []
    for sel in range(2):
        dist = (sel * w + a - j).astype(np.float32)
        valid = (dist >= 0) & (dist <= w)
        tabs.append(np.where(valid, -dilation * dist, NEG_INF))
    return np.stack(tabs).astype(np.float32)


def _alibi_slope_table():
    w = ATT_BLOCK
    tabs = np.zeros((N_HEADS // 2, 2 * w, w), np.float32)
    for h in range(N_HEADS):
        slope = 2.0 ** (-8.0 * (h + 1) / N_HEADS)
        tabs[h // 2, (h % 2) * w:(h % 2 + 1) * w, :] = slope * np.eye(w, dtype=np.float32)
    return tabs


def _aligned(row, align):
    return row if isinstance(row, int) else pl.multiple_of(row, align)


def _pair_block(qp, kp, vp, slopes, dist_t, low_half):
    w = ATT_BLOCK
    kw = kp.shape[0]
    zero = jnp.zeros((), BF16)
    qs = jnp.concatenate([jnp.where(low_half, qp, zero), jnp.where(low_half, zero, qp)], axis=0)
    lhs = jnp.concatenate([qs, slopes], axis=1)
    rhs = jnp.concatenate([kp, dist_t], axis=1)
    s2 = lax.dot_general(lhs, rhs, (((1,), (1,)), ((), ())), preferred_element_type=F32)
    v_ones = jnp.concatenate([vp, jnp.ones((kw, LANES), BF16)], axis=1)
    res, maxes = [], []
    for hh in range(2):
        s = s2[hh * w:(hh + 1) * w]
        m = jnp.max(s, axis=-1, keepdims=True)
        p = jnp.exp2(((s - m) * LOG2E).astype(BF16))
        res.append(jnp.dot(p, v_ones, preferred_element_type=F32))
        maxes.append(m)
    num = jnp.where(low_half, res[0][:, :LANES], res[1][:, :LANES])
    den = jnp.where(low_half, res[0][:, LANES:], res[1][:, LANES:])
    return num, den, jnp.where(low_half, maxes[0], maxes[1])


def _attn_body(q_ref, k_ref, v_ref, slope_ref, b1_ref, b4_ref, b16_ref, o_ref,
               f32_ref, sub4f_ref, sub4_ref, sub16_ref, num_ref, den_ref, max_ref, *, seq):
    w = ATT_BLOCK
    group = ATT_GROUP
    pair = pl.program_id(1)
    low_half = lax.broadcasted_iota(jnp.int32, (w, LANES), 1) < HEAD_DIM
    slopes = slope_ref[pair]
    srcs = (q_ref, k_ref, v_ref)
    n4, n16 = seq // 4, seq // 16
    for t in range(3):
        f32_ref[t] = srcs[t][...].astype(F32)
    for t in range(3):
        for r4 in range(4):
            sub4f_ref[t, r4 * n4:(r4 + 1) * n4, :] = f32_ref[t, pl.ds(r4, n4, stride=4), :]
    for t in range(3):
        sub4_ref[t] = sub4f_ref[t].astype(BF16)
        for r4 in range(4):
            for q4 in range(4):
                r16 = 4 * q4 + r4
                sub16_ref[t, r16 * n16:(r16 + 1) * n16, :] = (
                    sub4f_ref[t, pl.ds(r4 * n4 + q4, n16, stride=4), :].astype(BF16))

    def run_pattern(pidx, dilation, bias_ref, load):
        sub_len = seq // dilation
        blocks_per_class = sub_len // w

        def one_block(r, i):
            base = r * sub_len
            qrow = _aligned(base + i * w, w)
            if isinstance(i, int) and i == 0:
                kw, krow, sel = w, _aligned(base, w), 0
            elif isinstance(i, int):
                kw, krow, sel = 2 * w, _aligned(base + (i - 1) * w, w), 1
            else:
                kw = 2 * w
                krow = _aligned(base + jnp.clip((i - 1) * w, 0, sub_len - kw), w)
                sel = jnp.minimum(i, 1)
            num, den, m = _pair_block(load(0, qrow, w), load(1, krow, kw), load(2, krow, kw),
                                      slopes, bias_ref[sel, 0:kw, :], low_half)
            dst = pl.ds(dilation * w * i + r, w, stride=dilation) if dilation > 1 else pl.ds(qrow, w)
            num_ref[pidx, dst, :] = num
            den_ref[pidx, dst, :] = den
            max_ref[pidx, dst, :] = m

        def body(g, carry):
            for j in range(group):
                if group % blocks_per_class == 0:
                    per_trip = group // blocks_per_class
                    one_block(g * per_trip + j // blocks_per_class, j % blocks_per_class)
                else:
                    assert blocks_per_class % group == 0 and dilation == 1
                    one_block(0, g * group + j)
            return carry

        n_trips = seq // (w * group)
        if n_trips == 1:
            body(0, 0)
        else:
            lax.fori_loop(0, n_trips, body, 0)

    run_pattern(0, 1, b1_ref, lambda t, row, size: srcs[t][pl.ds(row, size), :])
    run_pattern(1, 4, b4_ref, lambda t, row, size: sub4_ref[t, pl.ds(row, size), :])
    run_pattern(2, 16, b16_ref, lambda t, row, size: sub16_ref[t, pl.ds(row, size), :])

    def merge(n, carry):
        rows = pl.ds(pl.multiple_of(n * w, w), w)
        m1, m2, m3 = max_ref[0, rows, :], max_ref[1, rows, :], max_ref[2, rows, :]
        mx = jnp.maximum(jnp.maximum(m1, m2), m3)
        e1, e2, e3 = jnp.exp(m1 - mx), jnp.exp(m2 - mx), jnp.exp(m3 - mx)
        num = e1 * num_ref[0, rows, :] + e2 * num_ref[1, rows, :] + e3 * num_ref[2, rows, :]
        den = e1 * den_ref[0, rows, :] + e2 * den_ref[1, rows, :] + e3 * den_ref[2, rows, :]
        o_ref[rows, :] = (num * (1.0 / den)).astype(BF16)
        return carry

    lax.fori_loop(0, seq // w, merge, 0)


def _attention(q, k, v):
    bsz, seq, _ = q.shape
    w = ATT_BLOCK
    slopes = jnp.asarray(_alibi_slope_table(), BF16)
    biases = [jnp.asarray(_alibi_distance_table(d, min(2 * w, seq // d)), BF16) for _, d in PATTERNS]
    blk = pl.BlockSpec((None, seq, LANES), lambda b, p: (b, 0, p))
    return pl.pallas_call(
        functools.partial(_attn_body, seq=seq),
        grid=(bsz, N_HEADS // 2),
        in_specs=[blk, blk, blk, _const_spec(slopes.shape)] + [_const_spec(t.shape) for t in biases],
        out_specs=blk,
        out_shape=jax.ShapeDtypeStruct((bsz, seq, ATT_WIDTH), BF16),
        scratch_shapes=[pltpu.VMEM((3, seq, LANES), F32),
                        pltpu.VMEM((3, seq, LANES), F32),
                        pltpu.VMEM((3, seq, LANES), BF16),
                        pltpu.VMEM((3, seq, LANES), BF16),
                        pltpu.VMEM((3, seq, LANES), F32),
                        pltpu.VMEM((3, seq, LANES), F32),
                        pltpu.VMEM((3, seq, LANES), F32)],
        compiler_params=_params(2),
        name="attn",
    )(q, k, v, slopes, *biases)


def _ssm_matrices(a_re, a_im, log_dt, b_re, b_im, c_re, c_im, d_skip):
    lc, g_n, n_st, ch = SSM_CHUNK, SSM_GROUPS, SSM_STATE, SSM_GROUP_CH
    hi = lax.Precision.HIGHEST
    lr, li = a_re.astype(F32), a_im.astype(F32)
    dt = jnp.exp(log_dt.astype(F32))[:, None]
    mag = jnp.exp(lr * dt)
    ang = li * dt
    ab_re, ab_im = mag * jnp.cos(ang), mag * jnp.sin(ang)
    nr, ni = ab_re - 1.0, ab_im
    den = lr * lr + li * li
    f_re = (nr * lr + ni * li) / den
    f_im = (ni * lr - nr * li) / den
    bb_re = f_re[..., None] * b_re - f_im[..., None] * b_im
    bb_im = f_re[..., None] * b_im + f_im[..., None] * b_re

    kk = jnp.arange(lc + 1, dtype=F32)[:, None, None]
    pw_mag = jnp.exp(kk * (lr * dt)[None])
    pw_re = pw_mag * jnp.cos(kk * ang[None])
    pw_im = pw_mag * jnp.sin(kk * ang[None])

    eye_g = jnp.eye(g_n, dtype=F32)

    cp_re = c_re[None] * pw_re[:, :, None, :] - c_im[None] * pw_im[:, :, None, :]
    cp_im = c_re[None] * pw_im[:, :, None, :] + c_im[None] * pw_re[:, :, None, :]

    taps = (jnp.einsum('kgcn,gni->kgci', cp_re[:lc], bb_re, precision=hi)
            - jnp.einsum('kgcn,gni->kgci', cp_im[:lc], bb_im, precision=hi))
    skip = d_skip.reshape(g_n, ch)
    taps = taps.at[0].add(skip[:, :, None] * jnp.eye(ch, dtype=F32)[None])
    kd = jnp.einsum('kgoi,gh->kgiho', taps, eye_g).reshape(lc, SSM_WIDTH, SSM_WIDTH)
    zero_blk = jnp.zeros((SSM_WIDTH, SSM_WIDTH), F32)
    toeplitz = jnp.concatenate(
        [jnp.concatenate([kd[j - s] if j >= s else zero_blk for j in range(lc)], axis=1)
         for s in range(lc)], axis=0)

    carry = jnp.stack([cp_re[1:], -cp_im[1:]], axis=0)
    carry = jnp.einsum('pjgcn,gh->pgnjhc', carry, eye_g).reshape(STATE_COLS, lc * SSM_WIDTH)
    wy = jnp.concatenate([carry, toeplitz], axis=0)

    rp_re, rp_im = pw_re[:lc][::-1], pw_im[:lc][::-1]
    bp_re = rp_re[..., None] * bb_re[None] - rp_im[..., None] * bb_im[None]
    bp_im = rp_re[..., None] * bb_im[None] + rp_im[..., None] * bb_re[None]
    bst = jnp.stack([bp_re, bp_im], axis=0)
    bst = jnp.einsum('psgnc,gh->sgcphn', bst, eye_g).reshape(lc * SSM_WIDTH, STATE_COLS)

    half = g_n * n_st
    a_pow = jnp.concatenate([pw_re[lc].reshape(1, half), pw_im[lc].reshape(1, half)], axis=1)
    return bst.astype(BF16), wy.astype(BF16), a_pow


def _ssm_body(u_ref, bst_ref, wy_ref, apow_ref, wglu_ref, bglu_ref, o_ref, st_ref,
              *, nb, n_chunks):
    lc = SSM_CHUNK
    half = STATE_COLS // 2
    rows = nb * n_chunks
    u2 = u_ref[...].reshape(rows, lc * SSM_WIDTH)
    st_ref[...] = jnp.dot(u2, bst_ref[...], preferred_element_type=F32)

    a_r = apow_ref[:, 0:half]
    a_i = apow_ref[:, half:]

    def step(c, carry):
        new = []
        for b in range(nb):
            x_r, x_i = carry[2 * b], carry[2 * b + 1]
            row = pl.ds(b * n_chunks + c, 1)
            s_r = st_ref[row, 0:half]
            s_i = st_ref[row, half:]
            st_ref[row, 0:half] = x_r
            st_ref[row, half:] = x_i
            new.append(a_r * x_r - a_i * x_i + s_r)
            new.append(a_r * x_i + a_i * x_r + s_i)
        return tuple(new)

    init = tuple(jnp.zeros((1, half), F32) for _ in range(2 * nb))
    lax.fori_loop(0, n_chunks, step, init)

    x_in = st_ref[...].astype(BF16)
    for j in range(lc):
        kdim = (j + 1) * SSM_WIDTH
        cols = slice(j * SSM_WIDTH, (j + 1) * SSM_WIDTH)
        y = (jnp.dot(x_in, wy_ref[0:STATE_COLS, cols], preferred_element_type=F32)
             + jnp.dot(u2[:, :kdim], wy_ref[STATE_COLS:STATE_COLS + kdim, cols],
                       preferred_element_type=F32))
        y = jax.nn.gelu(y)
        gate = jnp.dot(y.astype(BF16), wglu_ref[...], preferred_element_type=F32) + bglu_ref[...]
        o_ref[:, :, cols] = (y * jax.nn.sigmoid(gate)).astype(BF16).reshape(nb, n_chunks, SSM_WIDTH)


def _ssm(us, bst, wy, a_pow, w_glu, b_glu, nb):
    bsz, seq, _ = us.shape
    lc = SSM_CHUNK
    n_chunks = seq // lc
    u2 = us.reshape(bsz, n_chunks, lc * SSM_WIDTH)
    blk = pl.BlockSpec((nb, n_chunks, lc * SSM_WIDTH), lambda b: (b, 0, 0))
    out = pl.pallas_call(
        functools.partial(_ssm_body, nb=nb, n_chunks=n_chunks),
        grid=(bsz // nb,),
        in_specs=[blk, _const_spec(bst.shape), _const_spec(wy.shape), _const_spec(a_pow.shape),
                  _const_spec((SSM_WIDTH, SSM_WIDTH)), _const_spec((1, SSM_WIDTH))],
        out_specs=blk,
        out_shape=jax.ShapeDtypeStruct(u2.shape, BF16),
        scratch_shapes=[pltpu.VMEM((nb * n_chunks, STATE_COLS), F32)],
        compiler_params=_params(1),
        name="ssm",
    )(u2, bst, wy, a_pow, w_glu.astype(BF16), b_glu.reshape(1, SSM_WIDTH))
    return out.reshape(bsz, seq, SSM_WIDTH)


def _tail_body(x_ref, oatt_ref, sg_ref, sa_ref, ss_ref,
               mod_ref, wpa_ref, wps_ref, wo_ref, gf_ref, wup_ref, wc_ref, bc_ref,
               wdn_ref, gfin_ref, out_ref, carry_ref):
    tm = x_ref.shape[0]

    @pl.when(pl.program_id(1) == 0)
    def _():
        carry_ref[...] = jnp.zeros_like(carry_ref)

    y_att = jnp.dot(oatt_ref[...], wpa_ref[...], preferred_element_type=F32)
    y_ssm = jnp.dot(sg_ref[...], wps_ref[...], preferred_element_type=F32)
    merged = sa_ref[...].astype(F32) * y_att + ss_ref[...].astype(F32) * y_ssm
    mix = jnp.dot(merged.astype(BF16), wo_ref[...], preferred_element_type=F32)
    h1 = x_ref[...] + mod_ref[2:3, :] * mix

    u = _rms_modulate(h1, gf_ref[...], mod_ref[3:4, :], mod_ref[4:5, :]).astype(BF16)
    a = jnp.dot(u, wup_ref[:, 0:D_FF], preferred_element_type=F32)
    val = jnp.dot(u, wup_ref[:, D_FF:], preferred_element_type=F32)

    row = lax.broadcasted_iota(jnp.int32, a.shape, 0)
    prev1 = carry_ref[SUBLANES - 1:SUBLANES, :]
    prev2 = carry_ref[SUBLANES - 2:SUBLANES - 1, :]
    a1 = jnp.where(row == 0, prev1, pltpu.roll(a, 1, 0))
    a2 = jnp.where(row == 0, prev2, jnp.where(row == 1, prev1, pltpu.roll(a, 2, 0)))
    carry_ref[...] = a[tm - SUBLANES:, :]
    conv = bc_ref[...] + wc_ref[0:1, :] * a + wc_ref[1:2, :] * a1 + wc_ref[2:3, :] * a2
    act = (conv * jax.nn.sigmoid(conv) * val).astype(BF16)
    ffn = jnp.dot(act, wdn_ref[...], preferred_element_type=F32)
    h2 = h1 + mod_ref[5:6, :] * ffn

    ms = jnp.mean(h2 * h2, axis=-1, keepdims=True)
    out_ref[...] = (h2 * lax.rsqrt(ms + EPS)) * gfin_ref[...]


def _tail(x, o_att, sglu, sig_att, sig_ssm, mod3, w_proj_att, w_proj_ssm, w_out,
          g_ffn, w_up, w_conv, b_conv, w_down, g_final, tm):
    bsz, seq, d = x.shape

    def tok(width):
        return pl.BlockSpec((None, tm, width), lambda b, s: (b, s, 0))

    in_specs = [tok(d), tok(ATT_WIDTH), tok(SSM_WIDTH), tok(d), tok(d),
                pl.BlockSpec((None, 6, d), lambda b, s: (b, 0, 0)),
                _const_spec((ATT_WIDTH, d)), _const_spec((SSM_WIDTH, d)), _const_spec((d, d)),
                _const_spec((1, d)), _const_spec((d, 2 * D_FF)), _const_spec((CONV_W, D_FF)),
                _const_spec((1, D_FF)), _const_spec((D_FF, d)), _const_spec((1, d))]
    return pl.pallas_call(
        _tail_body,
        grid=(bsz, seq // tm),
        in_specs=in_specs,
        out_specs=tok(d),
        out_shape=jax.ShapeDtypeStruct((bsz, seq, d), F32),
        scratch_shapes=[pltpu.VMEM((SUBLANES, D_FF), F32)],
        compiler_params=_params(2),
        name="tail",
    )(x, o_att, sglu, sig_att, sig_ssm, mod3,
      w_proj_att.astype(BF16), w_proj_ssm.astype(BF16), w_out.astype(BF16),
      g_ffn.reshape(1, d), w_up.astype(BF16), w_conv, b_conv.reshape(1, D_FF),
      w_down.astype(BF16), g_final.reshape(1, d))


def kernel(x, c, w_ada, b_ada, g_mix, w_in, b_gate, a_re, a_im, log_dt, b_re, b_im, c_re, c_im,
           d_skip, w_glu, b_glu, w_proj_att, w_proj_ssm, w_out, g_ffn, w_up, w_conv, b_conv,
           w_down, g_final):
    depth = w_ada.shape[0]
    assert depth == 1, "the final RMSNorm is fused into the single layer's tail kernel"
    bsz, seq, d = x.shape
    l = 0
    mod3 = _ada(c, w_ada[l], b_ada[l]).reshape(bsz, 6, d)
    q, k, v, us, sig_att, sig_ssm = _inproj(x, mod3, g_mix[l], w_in[l], b_gate[l], tm=512)

    o_att = _attention(q, k, v)

    bst, wy, a_pow = _ssm_matrices(a_re[l], a_im[l], log_dt[l], b_re[l], b_im[l], c_re[l],
                                   c_im[l], d_skip[l])
    sglu = _ssm(us, bst, wy, a_pow, w_glu[l], b_glu[l], nb=2)

    return _tail(x, o_att, sglu, sig_att, sig_ssm, mod3, w_proj_att[l], w_proj_ssm[l],
                 w_out[l], g_ffn[l], w_up[l], w_conv[l], b_conv[l], w_down[l], g_final, tm=512)
```

```python
import functools
import math

import jax
import jax.numpy as jnp
import numpy as np
from jax import lax
from jax.experimental import pallas as pl
from jax.experimental.pallas import tpu as pltpu

D_MODEL = 1024
N_HEADS = 8
HEAD_DIM = 64
ATT_WIDTH = N_HEADS * HEAD_DIM
PATTERNS = ((128, 1), (512, 4), (2048, 16))
SSM_GROUPS = 16
SSM_GROUP_CH = 16
SSM_WIDTH = SSM_GROUPS * SSM_GROUP_CH
SSM_STATE = 64
D_FF = 2048
CONV_W = 3
EPS = 1e-6
NEG_INF = -1e30
LOG2E = math.log2(math.e)

LANES = 128
SUBLANES = 8
VMEM_LIMIT_BYTES = 56 * 1024 * 1024

ATT_BLOCK = 128
ATT_GROUP = 16
SSM_CHUNK = 8
STATE_COLS = 2 * SSM_GROUPS * SSM_STATE

BF16 = jnp.bfloat16
F32 = jnp.float32


def _const_spec(shape):
    zeros = (0,) * len(shape)
    return pl.BlockSpec(shape, lambda *_: zeros, pipeline_mode=pl.Buffered(1))


def _params(n_axes):
    return pltpu.CompilerParams(
        dimension_semantics=("arbitrary",) * n_axes,
        vmem_limit_bytes=VMEM_LIMIT_BYTES)


def _ada_body(c_ref, w_ref, b_ref, o_ref):
    c = c_ref[...]
    act = (c * jax.nn.sigmoid(c)).astype(BF16)
    o_ref[...] = jnp.dot(act, w_ref[...], preferred_element_type=F32) + b_ref[...]


def _ada(c, w_ada, b_ada):
    bsz = c.shape[0]
    n_out = w_ada.shape[1]
    tn = 1536
    return pl.pallas_call(
        _ada_body,
        grid=(n_out // tn,),
        in_specs=[_const_spec((bsz, D_MODEL)),
                  pl.BlockSpec((D_MODEL, tn), lambda j: (0, j)),
                  pl.BlockSpec((1, tn), lambda j: (0, j))],
        out_specs=pl.BlockSpec((bsz, tn), lambda j: (0, j)),
        out_shape=jax.ShapeDtypeStruct((bsz, n_out), F32),
        compiler_params=_params(1),
        name="ada",
    )(c, w_ada.astype(BF16), b_ada.reshape(1, n_out))


def _rms_modulate(x, gain, shift, scale):
    ms = jnp.mean(x * x, axis=-1, keepdims=True)
    return (x * lax.rsqrt(ms + EPS)) * (gain * (1.0 + scale)) + shift


def _inproj_body(x_ref, mod_ref, g_ref, w_ref, bg_ref,
                 q_ref, k_ref, v_ref, us_ref, sa_ref, ss_ref, us_scr):
    u = _rms_modulate(x_ref[...], g_ref[...], mod_ref[0:1, :], mod_ref[1:2, :]).astype(BF16)

    def proj(lo, hi):
        return jnp.dot(u, w_ref[:, lo:hi], preferred_element_type=F32)

    a = ATT_WIDTH
    q_ref[...] = (proj(0, a) * (HEAD_DIM ** -0.5)).astype(BF16)
    k_ref[...] = proj(a, 2 * a).astype(BF16)
    v_ref[...] = proj(2 * a, 3 * a).astype(BF16)
    o = 3 * a
    us = proj(o, o + SSM_WIDTH)
    n_tiles = SSM_WIDTH // LANES
    for h in range(n_tiles):
        us_scr[h] = us[:, h * LANES:(h + 1) * LANES]
    n_rows = us_scr.shape[1] // SSM_CHUNK
    us_ref[...] = jnp.concatenate(
        [us_scr[h, pl.ds(s, n_rows, stride=SSM_CHUNK), :]
         for s in range(SSM_CHUNK) for h in range(n_tiles)], axis=1).astype(BF16)
    o += SSM_WIDTH
    sa_ref[...] = jax.nn.sigmoid(proj(o, o + D_MODEL) + bg_ref[:, 0:D_MODEL]).astype(BF16)
    o += D_MODEL
    ss_ref[...] = jax.nn.sigmoid(proj(o, o + D_MODEL) + bg_ref[:, D_MODEL:]).astype(BF16)


def _inproj(x, mod3, g_mix, w_in, b_gate, tm):
    bsz, seq, d = x.shape
    in_width = w_in.shape[1]

    def tok(width):
        return pl.BlockSpec((None, tm, width), lambda b, s: (b, s, 0))

    def out(width):
        return jax.ShapeDtypeStruct((bsz, seq, width), BF16)

    return pl.pallas_call(
        _inproj_body,
        grid=(bsz, seq // tm),
        in_specs=[tok(d),
                  pl.BlockSpec((None, 6, d), lambda b, s: (b, 0, 0)),
                  _const_spec((1, d)),
                  _const_spec((d, in_width)),
                  _const_spec((1, 2 * d))],
        out_specs=[tok(ATT_WIDTH), tok(ATT_WIDTH), tok(ATT_WIDTH),
                   pl.BlockSpec((None, tm // SSM_CHUNK, SSM_CHUNK * SSM_WIDTH), lambda b, s: (b, s, 0)),
                   tok(d), tok(d)],
        out_shape=[out(ATT_WIDTH), out(ATT_WIDTH), out(ATT_WIDTH),
                   jax.ShapeDtypeStruct((bsz, seq // SSM_CHUNK, SSM_CHUNK * SSM_WIDTH), BF16),
                   out(d), out(d)],
        scratch_shapes=[pltpu.VMEM((SSM_WIDTH // LANES, tm, LANES), F32)],
        compiler_params=_params(2),
        name="inproj",
    )(x, mod3, g_mix.reshape(1, d), w_in.astype(BF16), b_gate.reshape(1, 2 * d))


def _alibi_distance_table(dilation, kw):
    w = ATT_BLOCK
    j = np.arange(kw)[:, None]
    a = np.arange(w)[None, :]
    tabs = []
    for sel in range(2):
        dist = (sel * w + a - j).astype(np.float32)
        valid = (dist >= 0) & (dist <= w)
        tabs.append(np.where(valid, -dilation * dist, NEG_INF))
    return np.stack(tabs).astype(np.float32)


def _alibi_slope_table():
    w = ATT_BLOCK
    tabs = np.zeros((N_HEADS // 2, 2 * w, w), np.float32)
    for h in range(N_HEADS):
        slope = 2.0 ** (-8.0 * (h + 1) / N_HEADS)
        tabs[h // 2, (h % 2) * w:(h % 2 + 1) * w, :] = slope * np.eye(w, dtype=np.float32)
    return tabs


def _aligned(row, align):
    return row if isinstance(row, int) else pl.multiple_of(row, align)


def _pair_block(qp, kp, vp, slopes, dist_t, low_half):
    w = ATT_BLOCK
    kw = kp.shape[0]
    zero = jnp.zeros((), BF16)
    qs = jnp.concatenate([jnp.where(low_half, qp, zero), jnp.where(low_half, zero, qp)], axis=0)
    lhs = jnp.concatenate([qs, slopes], axis=1)
    rhs = jnp.concatenate([kp, dist_t], axis=1)
    s2 = lax.dot_general(lhs, rhs, (((1,), (1,)), ((), ())), preferred_element_type=F32)
    v_ones = jnp.concatenate([vp, jnp.ones((kw, LANES), BF16)], axis=1)
    res, maxes = [], []
    for hh in range(2):
        s = s2[hh * w:(hh + 1) * w]
        m = jnp.max(s, axis=-1, keepdims=True)
        p = jnp.exp2(((s - m) * LOG2E).astype(BF16))
        res.append(jnp.dot(p, v_ones, preferred_element_type=F32))
        maxes.append(m)
    num = jnp.where(low_half, res[0][:, :LANES], res[1][:, :LANES])
    den = jnp.where(low_half, res[0][:, LANES:], res[1][:, LANES:])
    return num, den, jnp.where(low_half, maxes[0], maxes[1])


def _attn_body(q_ref, k_ref, v_ref, slope_ref, b1_ref, b4_ref, b16_ref, o_ref,
               f32_ref, sub4f_ref, sub4_ref, sub16_ref, num_ref, den_ref, max_ref, *, seq):
    w = ATT_BLOCK
    group = ATT_GROUP
    pair = pl.program_id(1)
    low_half = lax.broadcasted_iota(jnp.int32, (w, LANES), 1) < HEAD_DIM
    slopes = slope_ref[pair]
    srcs = (q_ref, k_ref, v_ref)
    n4, n16 = seq // 4, seq // 16
    for t in range(3):
        f32_ref[t] = srcs[t][...].astype(F32)
    for t in range(3):
        for r4 in range(4):
            sub4f_ref[t, r4 * n4:(r4 + 1) * n4, :] = f32_ref[t, pl.ds(r4, n4, stride=4), :]
    for t in range(3):
        sub4_ref[t] = sub4f_ref[t].astype(BF16)
        for r4 in range(4):
            for q4 in range(4):
                r16 = 4 * q4 + r4
                sub16_ref[t, r16 * n16:(r16 + 1) * n16, :] = (
                    sub4f_ref[t, pl.ds(r4 * n4 + q4, n16, stride=4), :].astype(BF16))

    def run_pattern(pidx, dilation, bias_ref, load):
        sub_len = seq // dilation
        blocks_per_class = sub_len // w

        def one_block(r, i):
            base = r * sub_len
            qrow = _aligned(base + i * w, w)
            if isinstance(i, int) and i == 0:
                kw, krow, sel = w, _aligned(base, w), 0
            elif isinstance(i, int):
                kw, krow, sel = 2 * w, _aligned(base + (i - 1) * w, w), 1
            else:
                kw = 2 * w
                krow = _aligned(base + jnp.clip((i - 1) * w, 0, sub_len - kw), w)
                sel = jnp.minimum(i, 1)
            num, den, m = _pair_block(load(0, qrow, w), load(1, krow, kw), load(2, krow, kw),
                                      slopes, bias_ref[sel, 0:kw, :], low_half)
            dst = pl.ds(dilation * w * i + r, w, stride=dilation) if dilation > 1 else pl.ds(qrow, w)
            num_ref[pidx, dst, :] = num
            den_ref[pidx, dst, :] = den
            max_ref[pidx, dst, :] = m

        def body(g, carry):
            for j in range(group):
                if group % blocks_per_class == 0:
                    per_trip = group // blocks_per_class
                    one_block(g * per_trip + j // blocks_per_class, j % blocks_per_class)
                else:
                    assert blocks_per_class % group == 0 and dilation == 1
                    one_block(0, g * group + j)
            return carry

        n_trips = seq // (w * group)
        if n_trips == 1:
            body(0, 0)
        else:
            lax.fori_loop(0, n_trips, body, 0)

    run_pattern(0, 1, b1_ref, lambda t, row, size: srcs[t][pl.ds(row, size), :])
    run_pattern(1, 4, b4_ref, lambda t, row, size: sub4_ref[t, pl.ds(row, size), :])
    run_pattern(2, 16, b16_ref, lambda t, row, size: sub16_ref[t, pl.ds(row, size), :])

    def merge(n, carry):
        rows = pl.ds(pl.multiple_of(n * w, w), w)
        m1, m2, m3 = max_ref[0, rows, :], max_ref[1, rows, :], max_ref[2, rows, :]
        mx = jnp.maximum(jnp.maximum(m1, m2), m3)
        e1, e2, e3 = jnp.exp(m1 - mx), jnp.exp(m2 - mx), jnp.exp(m3 - mx)
        num = e1 * num_ref[0, rows, :] + e2 * num_ref[1, rows, :] + e3 * num_ref[2, rows, :]
        den = e1 * den_ref[0, rows, :] + e2 * den_ref[1, rows, :] + e3 * den_ref[2, rows, :]
        o_ref[rows, :] = (num * (1.0 / den)).astype(BF16)
        return carry

    lax.fori_loop(0, seq // w, merge, 0)


def _attention(q, k, v):
    bsz, seq, _ = q.shape
    w = ATT_BLOCK
    slopes = jnp.asarray(_alibi_slope_table(), BF16)
    biases = [jnp.asarray(_alibi_distance_table(d, min(2 * w, seq // d)), BF16) for _, d in PATTERNS]
    blk = pl.BlockSpec((None, seq, LANES), lambda b, p: (b, 0, p))
    return pl.pallas_call(
        functools.partial(_attn_body, seq=seq),
        grid=(bsz, N_HEADS // 2),
        in_specs=[blk, blk, blk, _const_spec(slopes.shape)] + [_const_spec(t.shape) for t in biases],
        out_specs=blk,
        out_shape=jax.ShapeDtypeStruct((bsz, seq, ATT_WIDTH), BF16),
        scratch_shapes=[pltpu.VMEM((3, seq, LANES), F32),
                        pltpu.VMEM((3, seq, LANES), F32),
                        pltpu.VMEM((3, seq, LANES), BF16),
                        pltpu.VMEM((3, seq, LANES), BF16),
                        pltpu.VMEM((3, seq, LANES), F32),
                        pltpu.VMEM((3, seq, LANES), F32),
                        pltpu.VMEM((3, seq, LANES), F32)],
        compiler_params=_params(2),
        name="attn",
    )(q, k, v, slopes, *biases)


def _ssm_matrices(a_re, a_im, log_dt, b_re, b_im, c_re, c_im, d_skip):
    lc, g_n, n_st, ch = SSM_CHUNK, SSM_GROUPS, SSM_STATE, SSM_GROUP_CH
    hi = lax.Precision.HIGHEST
    lr, li = a_re.astype(F32), a_im.astype(F32)
    dt = jnp.exp(log_dt.astype(F32))[:, None]
    mag = jnp.exp(lr * dt)
    ang = li * dt
    ab_re, ab_im = mag * jnp.cos(ang), mag * jnp.sin(ang)
    nr, ni = ab_re - 1.0, ab_im
    den = lr * lr + li * li
    f_re = (nr * lr + ni * li) / den
    f_im = (ni * lr - nr * li) / den
    bb_re = f_re[..., None] * b_re - f_im[..., None] * b_im
    bb_im = f_re[..., None] * b_im + f_im[..., None] * b_re

    kk = jnp.arange(lc + 1, dtype=F32)[:, None, None]
    pw_mag = jnp.exp(kk * (lr * dt)[None])
    pw_re = pw_mag * jnp.cos(kk * ang[None])
    pw_im = pw_mag * jnp.sin(kk * ang[None])

    eye_g = jnp.eye(g_n, dtype=F32)

    cp_re = c_re[None] * pw_re[:, :, None, :] - c_im[None] * pw_im[:, :, None, :]
    cp_im = c_re[None] * pw_im[:, :, None, :] + c_im[None] * pw_re[:, :, None, :]

    taps = (jnp.einsum('kgcn,gni->kgci', cp_re[:lc], bb_re, precision=hi)
            - jnp.einsum('kgcn,gni->kgci', cp_im[:lc], bb_im, precision=hi))
    skip = d_skip.reshape(g_n, ch)
    taps = taps.at[0].add(skip[:, :, None] * jnp.eye(ch, dtype=F32)[None])
    kd = jnp.einsum('kgoi,gh->kgiho', taps, eye_g).reshape(lc, SSM_WIDTH, SSM_WIDTH)
    zero_blk = jnp.zeros((SSM_WIDTH, SSM_WIDTH), F32)
    toeplitz = jnp.concatenate(
        [jnp.concatenate([kd[j - s] if j >= s else zero_blk for j in range(lc)], axis=1)
         for s in range(lc)], axis=0)

    carry = jnp.stack([cp_re[1:], -cp_im[1:]], axis=0)
    carry = jnp.einsum('pjgcn,gh->pgnjhc', carry, eye_g).reshape(STATE_COLS, lc * SSM_WIDTH)
    wy = jnp.concatenate([carry, toeplitz], axis=0)

    rp_re, rp_im = pw_re[:lc][::-1], pw_im[:lc][::-1]
    bp_re = rp_re[..., None] * bb_re[None] - rp_im[..., None] * bb_im[None]
    bp_im = rp_re[..., None] * bb_im[None] + rp_im[..., None] * bb_re[None]
    bst = jnp.stack([bp_re, bp_im], axis=0)
    bst = jnp.einsum('psgnc,gh->sgcphn', bst, eye_g).reshape(lc * SSM_WIDTH, STATE_COLS)

    half = g_n * n_st
    a_pow = jnp.concatenate([pw_re[lc].reshape(1, half), pw_im[lc].reshape(1, half)], axis=1)
    return bst.astype(BF16), wy.astype(BF16), a_pow


def _ssm_body(u_ref, bst_ref, wy_ref, apow_ref, wglu_ref, bglu_ref, o_ref, st_ref, y_ref, tok_ref,
              *, nb, n_chunks):
    lc = SSM_CHUNK
    half = STATE_COLS // 2
    rows = nb * n_chunks
    u2 = u_ref[...].reshape(rows, lc * SSM_WIDTH)
    st_ref[...] = jnp.dot(u2, bst_ref[...], preferred_element_type=F32)

    a_r = apow_ref[:, 0:half]
    a_i = apow_ref[:, half:]

    def step(c, carry):
        new = []
        for b in range(nb):
            x_r, x_i = carry[2 * b], carry[2 * b + 1]
            row = pl.ds(b * n_chunks + c, 1)
            s_r = st_ref[row, 0:half]
            s_i = st_ref[row, half:]
            st_ref[row, 0:half] = x_r
            st_ref[row, half:] = x_i
            new.append(a_r * x_r - a_i * x_i + s_r)
            new.append(a_r * x_i + a_i * x_r + s_i)
        return tuple(new)

    init = tuple(jnp.zeros((1, half), F32) for _ in range(2 * nb))
    lax.fori_loop(0, n_chunks, step, init)

    x_in = st_ref[...].astype(BF16)
    hw = lc * SSM_WIDTH // 2
    for half_idx in range(2):
        cols = slice(half_idx * hw, (half_idx + 1) * hw)
        kdim = (half_idx + 1) * hw
        y_ref[:, cols] = (
            jnp.dot(x_in, wy_ref[0:STATE_COLS, cols], preferred_element_type=F32)
            + jnp.dot(u2[:, :kdim], wy_ref[STATE_COLS:STATE_COLS + kdim, cols],
                      preferred_element_type=F32))
    for j in range(lc):
        y = jax.nn.gelu(y_ref[:, j * SSM_WIDTH:(j + 1) * SSM_WIDTH])
        gate = jnp.dot(y.astype(BF16), wglu_ref[...], preferred_element_type=F32) + bglu_ref[...]
        res = y * jax.nn.sigmoid(gate)
        for h in range(SSM_WIDTH // LANES):
            tok_ref[h, pl.ds(j, rows, stride=lc), :] = res[:, h * LANES:(h + 1) * LANES]
    out = jnp.concatenate([tok_ref[h] for h in range(SSM_WIDTH // LANES)], axis=1)
    o_ref[...] = out.astype(BF16).reshape(nb, n_chunks * lc, SSM_WIDTH)


def _ssm(u2, bst, wy, a_pow, w_glu, b_glu, nb):
    bsz, n_chunks, _ = u2.shape
    lc = SSM_CHUNK
    seq = n_chunks * lc
    return pl.pallas_call(
        functools.partial(_ssm_body, nb=nb, n_chunks=n_chunks),
        grid=(bsz // nb,),
        in_specs=[pl.BlockSpec((nb, n_chunks, lc * SSM_WIDTH), lambda b: (b, 0, 0)),
                  _const_spec(bst.shape), _const_spec(wy.shape), _const_spec(a_pow.shape),
                  _const_spec((SSM_WIDTH, SSM_WIDTH)), _const_spec((1, SSM_WIDTH))],
        out_specs=pl.BlockSpec((nb, seq, SSM_WIDTH), lambda b: (b, 0, 0)),
        out_shape=jax.ShapeDtypeStruct((bsz, seq, SSM_WIDTH), BF16),
        scratch_shapes=[pltpu.VMEM((nb * n_chunks, STATE_COLS), F32),
                        pltpu.VMEM((nb * n_chunks, lc * SSM_WIDTH), F32),
                        pltpu.VMEM((SSM_WIDTH // LANES, nb * seq, LANES), F32)],
        compiler_params=_params(1),
        name="ssm",
    )(u2, bst, wy, a_pow, w_glu.astype(BF16), b_glu.reshape(1, SSM_WIDTH))


def _tail_body(x_ref, oatt_ref, sg_ref, sa_ref, ss_ref,
               mod_ref, wpa_ref, wps_ref, wo_ref, gf_ref, wup_ref, wc_ref, bc_ref,
               wdn_ref, gfin_ref, out_ref, carry_ref):
    tm = x_ref.shape[0]

    @pl.when(pl.program_id(1) == 0)
    def _():
        carry_ref[...] = jnp.zeros_like(carry_ref)

    y_att = jnp.dot(oatt_ref[...], wpa_ref[...], preferred_element_type=F32)
    y_ssm = jnp.dot(sg_ref[...], wps_ref[...], preferred_element_type=F32)
    merged = sa_ref[...].astype(F32) * y_att + ss_ref[...].astype(F32) * y_ssm
    mix = jnp.dot(merged.astype(BF16), wo_ref[...], preferred_element_type=F32)
    h1 = x_ref[...] + mod_ref[2:3, :] * mix

    u = _rms_modulate(h1, gf_ref[...], mod_ref[3:4, :], mod_ref[4:5, :]).astype(BF16)
    a = jnp.dot(u, wup_ref[:, 0:D_FF], preferred_element_type=F32)
    val = jnp.dot(u, wup_ref[:, D_FF:], preferred_element_type=F32)

    row = lax.broadcasted_iota(jnp.int32, a.shape, 0)
    prev1 = carry_ref[SUBLANES - 1:SUBLANES, :]
    prev2 = carry_ref[SUBLANES - 2:SUBLANES - 1, :]
    a1 = jnp.where(row == 0, prev1, pltpu.roll(a, 1, 0))
    a2 = jnp.where(row == 0, prev2, jnp.where(row == 1, prev1, pltpu.roll(a, 2, 0)))
    carry_ref[...] = a[tm - SUBLANES:, :]
    conv = bc_ref[...] + wc_ref[0:1, :] * a + wc_ref[1:2, :] * a1 + wc_ref[2:3, :] * a2
    act = (conv * jax.nn.sigmoid(conv) * val).astype(BF16)
    ffn = jnp.dot(act, wdn_ref[...], preferred_element_type=F32)
    h2 = h1 + mod_ref[5:6, :] * ffn

    ms = jnp.mean(h2 * h2, axis=-1, keepdims=True)
    out_ref[...] = (h2 * lax.rsqrt(ms + EPS)) * gfin_ref[...]


def _tail(x, o_att, sglu, sig_att, sig_ssm, mod3, w_proj_att, w_proj_ssm, w_out,
          g_ffn, w_up, w_conv, b_conv, w_down, g_final, tm):
    bsz, seq, d = x.shape

    def tok(width):
        return pl.BlockSpec((None, tm, width), lambda b, s: (b, s, 0))

    in_specs = [tok(d), tok(ATT_WIDTH), tok(SSM_WIDTH), tok(d), tok(d),
                pl.BlockSpec((None, 6, d), lambda b, s: (b, 0, 0)),
                _const_spec((ATT_WIDTH, d)), _const_spec((SSM_WIDTH, d)), _const_spec((d, d)),
                _const_spec((1, d)), _const_spec((d, 2 * D_FF)), _const_spec((CONV_W, D_FF)),
                _const_spec((1, D_FF)), _const_spec((D_FF, d)), _const_spec((1, d))]
    return pl.pallas_call(
        _tail_body,
        grid=(bsz, seq // tm),
        in_specs=in_specs,
        out_specs=tok(d),
        out_shape=jax.ShapeDtypeStruct((bsz, seq, d), F32),
        scratch_shapes=[pltpu.VMEM((SUBLANES, D_FF), F32)],
        compiler_params=_params(2),
        name="tail",
    )(x, o_att, sglu, sig_att, sig_ssm, mod3,
      w_proj_att.astype(BF16), w_proj_ssm.astype(BF16), w_out.astype(BF16),
      g_ffn.reshape(1, d), w_up.astype(BF16), w_conv, b_conv.reshape(1, D_FF),
      w_down.astype(BF16), g_final.reshape(1, d))


def kernel(x, c, w_ada, b_ada, g_mix, w_in, b_gate, a_re, a_im, log_dt, b_re, b_im, c_re, c_im,
           d_skip, w_glu, b_glu, w_proj_att, w_proj_ssm, w_out, g_ffn, w_up, w_conv, b_conv,
           w_down, g_final):
    depth = w_ada.shape[0]
    assert depth == 1, "the final RMSNorm is fused into the single layer's tail kernel"
    bsz, seq, d = x.shape
    l = 0
    mod3 = _ada(c, w_ada[l], b_ada[l]).reshape(bsz, 6, d)
    q, k, v, us, sig_att, sig_ssm = _inproj(x, mod3, g_mix[l], w_in[l], b_gate[l], tm=1024)

    o_att = _attention(q, k, v)

    bst, wy, a_pow = _ssm_matrices(a_re[l], a_im[l], log_dt[l], b_re[l], b_im[l], c_re[l],
                                   c_im[l], d_skip[l])
    sglu = _ssm(us, bst, wy, a_pow, w_glu[l], b_glu[l], nb=2)

    return _tail(x, o_att, sglu, sig_att, sig_ssm, mod3, w_proj_att[l], w_proj_ssm[l],
                 w_out[l], g_ffn[l], w_up[l], w_conv[l], b_conv[l], w_down[l], g_final, tm=512)
```

```python
import functools
import math

import jax
import jax.numpy as jnp
import numpy as np
from jax import lax
from jax.experimental import pallas as pl
from jax.experimental.pallas import tpu as pltpu

D_MODEL = 1024
N_HEADS = 8
HEAD_DIM = 64
ATT_WIDTH = N_HEADS * HEAD_DIM
PATTERNS = ((128, 1), (512, 4), (2048, 16))
SSM_GROUPS = 16
SSM_GROUP_CH = 16
SSM_WIDTH = SSM_GROUPS * SSM_GROUP_CH
SSM_STATE = 64
D_FF = 2048
CONV_W = 3
EPS = 1e-6
NEG_INF = -1e30
LOG2E = math.log2(math.e)

LANES = 128
SUBLANES = 8
VMEM_LIMIT_BYTES = 56 * 1024 * 1024

ATT_BLOCK = 128
SSM_CHUNK = 8
STATE_COLS = 2 * SSM_GROUPS * SSM_STATE

BF16 = jnp.bfloat16
F32 = jnp.float32


def _const_spec(shape):
    zeros = (0,) * len(shape)
    return pl.BlockSpec(shape, lambda *_: zeros, pipeline_mode=pl.Buffered(1))


def _params(n_axes):
    return pltpu.CompilerParams(
        dimension_semantics=("arbitrary",) * n_axes,
        vmem_limit_bytes=VMEM_LIMIT_BYTES)


def _ada_body(c_ref, w_ref, b_ref, o_ref):
    c = c_ref[...]
    act = (c * jax.nn.sigmoid(c)).astype(BF16)
    o_ref[...] = jnp.dot(act, w_ref[...], preferred_element_type=F32) + b_ref[...]


def _ada(c, w_ada, b_ada):
    bsz = c.shape[0]
    n_out = w_ada.shape[1]
    tn = 1536
    return pl.pallas_call(
        _ada_body,
        grid=(n_out // tn,),
        in_specs=[_const_spec((bsz, D_MODEL)),
                  pl.BlockSpec((D_MODEL, tn), lambda j: (0, j)),
                  pl.BlockSpec((1, tn), lambda j: (0, j))],
        out_specs=pl.BlockSpec((bsz, tn), lambda j: (0, j)),
        out_shape=jax.ShapeDtypeStruct((bsz, n_out), F32),
        compiler_params=_params(1),
        name="ada",
    )(c, w_ada.astype(BF16), b_ada.reshape(1, n_out))


def _rms_modulate(x, gain, shift, scale):
    ms = jnp.mean(x * x, axis=-1, keepdims=True)
    return (x * lax.rsqrt(ms + EPS)) * (gain * (1.0 + scale)) + shift


def _inproj_body(x_ref, mod_ref, g_ref, w_ref, bg_ref,
                 q_ref, k_ref, v_ref, us_ref, sa_ref, ss_ref, us_scr):
    u = _rms_modulate(x_ref[...], g_ref[...], mod_ref[0:1, :], mod_ref[1:2, :]).astype(BF16)

    def proj(lo, hi):
        return jnp.dot(u, w_ref[:, lo:hi], preferred_element_type=F32)

    a = ATT_WIDTH
    q_ref[...] = (proj(0, a) * (HEAD_DIM ** -0.5)).astype(BF16)
    k_ref[...] = proj(a, 2 * a).astype(BF16)
    v_ref[...] = proj(2 * a, 3 * a).astype(BF16)
    o = 3 * a
    us = proj(o, o + SSM_WIDTH)
    n_tiles = SSM_WIDTH // LANES
    for h in range(n_tiles):
        us_scr[h] = us[:, h * LANES:(h + 1) * LANES]
    n_rows = us_scr.shape[1] // SSM_CHUNK
    us_ref[...] = jnp.concatenate(
        [us_scr[h, pl.ds(s, n_rows, stride=SSM_CHUNK), :]
         for s in range(SSM_CHUNK) for h in range(n_tiles)], axis=1).astype(BF16)
    o += SSM_WIDTH
    sa_ref[...] = jax.nn.sigmoid(proj(o, o + D_MODEL) + bg_ref[:, 0:D_MODEL]).astype(BF16)
    o += D_MODEL
    ss_ref[...] = jax.nn.sigmoid(proj(o, o + D_MODEL) + bg_ref[:, D_MODEL:]).astype(BF16)


def _inproj(x, mod3, g_mix, w_in, b_gate, tm):
    bsz, seq, d = x.shape
    in_width = w_in.shape[1]

    def tok(width):
        return pl.BlockSpec((None, tm, width), lambda b, s: (b, s, 0))

    def out(width):
        return jax.ShapeDtypeStruct((bsz, seq, width), BF16)

    return pl.pallas_call(
        _inproj_body,
        grid=(bsz, seq // tm),
        in_specs=[tok(d),
                  pl.BlockSpec((None, 6, d), lambda b, s: (b, 0, 0)),
                  _const_spec((1, d)),
                  _const_spec((d, in_width)),
                  _const_spec((1, 2 * d))],
        out_specs=[tok(ATT_WIDTH), tok(ATT_WIDTH), tok(ATT_WIDTH),
                   pl.BlockSpec((None, tm // SSM_CHUNK, SSM_CHUNK * SSM_WIDTH), lambda b, s: (b, s, 0)),
                   tok(d), tok(d)],
        out_shape=[out(ATT_WIDTH), out(ATT_WIDTH), out(ATT_WIDTH),
                   jax.ShapeDtypeStruct((bsz, seq // SSM_CHUNK, SSM_CHUNK * SSM_WIDTH), BF16),
                   out(d), out(d)],
        scratch_shapes=[pltpu.VMEM((SSM_WIDTH // LANES, tm, LANES), F32)],
        compiler_params=_params(2),
        name="inproj",
    )(x, mod3, g_mix.reshape(1, d), w_in.astype(BF16), b_gate.reshape(1, 2 * d))


def _alibi_distance_table(dilation, kw):
    w = ATT_BLOCK
    j = np.arange(kw)[:, None]
    a = np.arange(w)[None, :]
    tabs = []
    for sel in range(2):
        dist = (sel * w + a - j).astype(np.float32)
        valid = (dist >= 0) & (dist <= w)
        tabs.append(np.where(valid, -dilation * dist, NEG_INF))
    return np.stack(tabs).astype(np.float32)


def _alibi_slope_table():
    w = ATT_BLOCK
    tabs = np.zeros((N_HEADS // 2, 2 * w, w), np.float32)
    for h in range(N_HEADS):
        slope = 2.0 ** (-8.0 * (h + 1) / N_HEADS)
        tabs[h // 2, (h % 2) * w:(h % 2 + 1) * w, :] = slope * np.eye(w, dtype=np.float32)
    return tabs


def _pair_block(qp, kp, vp, slopes, dist_t, low_half):
    w = ATT_BLOCK
    kw = kp.shape[0]
    zero = jnp.zeros((), BF16)
    qs = jnp.concatenate([jnp.where(low_half, qp, zero), jnp.where(low_half, zero, qp)], axis=0)
    lhs = jnp.concatenate([qs, slopes], axis=1)
    rhs = jnp.concatenate([kp, dist_t], axis=1)
    s2 = lax.dot_general(lhs, rhs, (((1,), (1,)), ((), ())), preferred_element_type=F32)
    v_ones = jnp.concatenate([vp, jnp.ones((kw, LANES), BF16)], axis=1)
    res, maxes = [], []
    for hh in range(2):
        s = s2[hh * w:(hh + 1) * w]
        m = jnp.max(s, axis=-1, keepdims=True)
        p = jnp.exp2(((s - m) * LOG2E).astype(BF16))
        res.append(jnp.dot(p, v_ones, preferred_element_type=F32))
        maxes.append(m)
    num = jnp.where(low_half, res[0][:, :LANES], res[1][:, :LANES])
    den = jnp.where(low_half, res[0][:, LANES:], res[1][:, LANES:])
    return num, den, jnp.where(low_half, maxes[0], maxes[1])


def _attn_body(q_ref, k_ref, v_ref, slope_ref, b1_ref, b4_ref, b16_ref, o_ref,
               f32_ref, sub4f_ref, sub4_ref, sub16_ref, num_ref, den_ref, max_ref, *, seq):
    w = ATT_BLOCK
    pair = pl.program_id(1)
    low_half = lax.broadcasted_iota(jnp.int32, (w, LANES), 1) < HEAD_DIM
    slopes = slope_ref[pair]
    srcs = (q_ref, k_ref, v_ref)
    n4, n16 = seq // 4, seq // 16
    for t in range(3):
        f32_ref[t] = srcs[t][...].astype(F32)
    for t in range(3):
        for r4 in range(4):
            sub4f_ref[t, r4 * n4:(r4 + 1) * n4, :] = f32_ref[t, pl.ds(r4, n4, stride=4), :]
    for t in range(3):
        sub4_ref[t] = sub4f_ref[t].astype(BF16)
        for r4 in range(4):
            for q4 in range(4):
                r16 = 4 * q4 + r4
                sub16_ref[t, r16 * n16:(r16 + 1) * n16, :] = (
                    sub4f_ref[t, pl.ds(r4 * n4 + q4, n16, stride=4), :].astype(BF16))

    def block(load, bias_ref, sub_len, r, i):
        base = r * sub_len
        kw = w if i == 0 else 2 * w
        krow = base + max(i - 1, 0) * w
        sel = min(i, 1)
        return _pair_block(load(0, base + i * w, w), load(1, krow, kw), load(2, krow, kw),
                           slopes, bias_ref[sel, 0:kw, :], low_half)

    dilated = ((0, 4, b4_ref, lambda t, row, size: sub4_ref[t, row:row + size, :]),
               (1, 16, b16_ref, lambda t, row, size: sub16_ref[t, row:row + size, :]))
    for pidx, dilation, bias_ref, load in dilated:
        sub_len = seq // dilation
        for r in range(dilation):
            for i in range(sub_len // w):
                num, den, m = block(load, bias_ref, sub_len, r, i)
                dst = pl.ds(dilation * w * i + r, w, stride=dilation)
                num_ref[pidx, dst, :] = num
                den_ref[pidx, dst, :] = den
                max_ref[pidx, dst, :] = m

    for i in range(seq // w):
        num1, den1, m1 = block(lambda t, row, size: srcs[t][row:row + size, :], b1_ref, seq, 0, i)
        rows = slice(i * w, (i + 1) * w)
        m2, m3 = max_ref[0, rows, :], max_ref[1, rows, :]
        mx = jnp.maximum(jnp.maximum(m1, m2), m3)
        e1, e2, e3 = jnp.exp(m1 - mx), jnp.exp(m2 - mx), jnp.exp(m3 - mx)
        num = e1 * num1 + e2 * num_ref[0, rows, :] + e3 * num_ref[1, rows, :]
        den = e1 * den1 + e2 * den_ref[0, rows, :] + e3 * den_ref[1, rows, :]
        o_ref[rows, :] = (num * (1.0 / den)).astype(BF16)


def _attention(q, k, v):
    bsz, seq, _ = q.shape
    w = ATT_BLOCK
    slopes = jnp.asarray(_alibi_slope_table(), BF16)
    biases = [jnp.asarray(_alibi_distance_table(d, min(2 * w, seq // d)), BF16) for _, d in PATTERNS]
    blk = pl.BlockSpec((None, seq, LANES), lambda b, p: (b, 0, p))
    return pl.pallas_call(
        functools.partial(_attn_body, seq=seq),
        grid=(bsz, N_HEADS // 2),
        in_specs=[blk, blk, blk, _const_spec(slopes.shape)] + [_const_spec(t.shape) for t in biases],
        out_specs=blk,
        out_shape=jax.ShapeDtypeStruct((bsz, seq, ATT_WIDTH), BF16),
        scratch_shapes=[pltpu.VMEM((3, seq, LANES), F32),
                        pltpu.VMEM((3, seq, LANES), F32),
                        pltpu.VMEM((3, seq, LANES), BF16),
                        pltpu.VMEM((3, seq, LANES), BF16),
                        pltpu.VMEM((2, seq, LANES), F32),
                        pltpu.VMEM((2, seq, LANES), F32),
                        pltpu.VMEM((2, seq, LANES), F32)],
        compiler_params=_params(2),
        name="attn",
    )(q, k, v, slopes, *biases)


def _placement(n_outer, n_inner):
    t = np.zeros((n_outer, n_inner, n_outer, SSM_GROUPS, n_inner), np.float32)
    for o in range(n_outer):
        for i in range(n_inner):
            t[o, i, o, :, i] = 1.0
    return t.reshape(n_outer * n_inner, n_outer * SSM_GROUPS * n_inner)


def _block_diagonal(val, n_outer, n_inner, row_inner):
    full = jnp.dot(val, jnp.asarray(_placement(n_outer, n_inner)),
                   precision=lax.Precision.HIGHEST)
    row_g = (lax.broadcasted_iota(jnp.int32, full.shape, 0) // row_inner) % SSM_GROUPS
    col_g = (lax.broadcasted_iota(jnp.int32, full.shape, 1) // n_inner) % SSM_GROUPS
    return jnp.where(row_g == col_g, full, 0.0).astype(BF16)


def _ssm_matrices(a_re, a_im, log_dt, b_re, b_im, c_re, c_im, d_skip):
    lc, g_n, n_st, ch = SSM_CHUNK, SSM_GROUPS, SSM_STATE, SSM_GROUP_CH
    hi = lax.Precision.HIGHEST
    lr, li = a_re.astype(F32), a_im.astype(F32)
    dt = jnp.exp(log_dt.astype(F32))[:, None]
    mag = jnp.exp(lr * dt)
    ang = li * dt
    ab_re, ab_im = mag * jnp.cos(ang), mag * jnp.sin(ang)
    nr, ni = ab_re - 1.0, ab_im
    den = lr * lr + li * li
    f_re = (nr * lr + ni * li) / den
    f_im = (ni * lr - nr * li) / den
    bb_re = f_re[..., None] * b_re - f_im[..., None] * b_im
    bb_im = f_re[..., None] * b_im + f_im[..., None] * b_re

    kk = jnp.arange(lc + 1, dtype=F32)[:, None, None]
    pw_mag = jnp.exp(kk * (lr * dt)[None])
    pw_re = pw_mag * jnp.cos(kk * ang[None])
    pw_im = pw_mag * jnp.sin(kk * ang[None])

    cp_re = c_re[None] * pw_re[:, :, None, :] - c_im[None] * pw_im[:, :, None, :]
    cp_im = c_re[None] * pw_im[:, :, None, :] + c_im[None] * pw_re[:, :, None, :]

    taps = (jnp.einsum('kgcn,gni->kgci', cp_re[:lc], bb_re, precision=hi)
            - jnp.einsum('kgcn,gni->kgci', cp_im[:lc], bb_im, precision=hi))
    skip = d_skip.reshape(g_n, ch)
    taps = taps.at[0].add(skip[:, :, None] * jnp.eye(ch, dtype=F32)[None])
    lag = np.arange(lc)[None, :] - np.arange(lc)[:, None]
    tz = taps[np.maximum(lag, 0)] * jnp.asarray(lag >= 0, F32)[:, :, None, None, None]
    tz = tz.transpose(0, 2, 4, 1, 3).reshape(lc * SSM_WIDTH, lc * ch)
    toeplitz = _block_diagonal(tz, lc, ch, ch)

    cv = jnp.stack([cp_re[1:], -cp_im[1:]], axis=0)
    cv = cv.transpose(0, 2, 4, 1, 3).reshape(STATE_COLS, lc * ch)
    carry = _block_diagonal(cv, lc, ch, n_st)

    rp_re, rp_im = pw_re[:lc][::-1], pw_im[:lc][::-1]
    bp_re = rp_re[..., None] * bb_re[None] - rp_im[..., None] * bb_im[None]
    bp_im = rp_re[..., None] * bb_im[None] + rp_im[..., None] * bb_re[None]
    bv = jnp.stack([bp_re, bp_im], axis=0)
    bv = bv.transpose(1, 2, 4, 0, 3).reshape(lc * SSM_WIDTH, 2 * n_st)
    bst = _block_diagonal(bv, 2, n_st, ch)

    half = g_n * n_st
    a_pow = jnp.concatenate([pw_re[lc].reshape(1, half), pw_im[lc].reshape(1, half)], axis=1)
    return bst, carry, toeplitz, a_pow


def _ssm_body(u_ref, bst_ref, wc_ref, wt_ref, apow_ref, wglu_ref, bglu_ref, o_ref,
              st_ref, y_ref, tok_ref, *, nb, n_chunks):
    lc = SSM_CHUNK
    half = STATE_COLS // 2
    rows = nb * n_chunks
    u2 = u_ref[...].reshape(rows, lc * SSM_WIDTH)
    st_ref[...] = jnp.dot(u2, bst_ref[...], preferred_element_type=F32)

    a_r = apow_ref[:, 0:half]
    a_i = apow_ref[:, half:]

    def step(c, carry):
        new = []
        for b in range(nb):
            x_r, x_i = carry[2 * b], carry[2 * b + 1]
            row = pl.ds(b * n_chunks + c, 1)
            s_r = st_ref[row, 0:half]
            s_i = st_ref[row, half:]
            st_ref[row, 0:half] = x_r
            st_ref[row, half:] = x_i
            new.append(a_r * x_r - a_i * x_i + s_r)
            new.append(a_r * x_i + a_i * x_r + s_i)
        return tuple(new)

    init = tuple(jnp.zeros((1, half), F32) for _ in range(2 * nb))
    lax.fori_loop(0, n_chunks, step, init)

    x_in = st_ref[...].astype(BF16)
    hw = lc * SSM_WIDTH // 2
    for half_idx in range(2):
        cols = slice(half_idx * hw, (half_idx + 1) * hw)
        kdim = (half_idx + 1) * hw
        y_ref[:, cols] = (
            jnp.dot(x_in, wc_ref[:, cols], preferred_element_type=F32)
            + jnp.dot(u2[:, :kdim], wt_ref[0:kdim, cols], preferred_element_type=F32))
    for j in range(lc):
        y = jax.nn.gelu(y_ref[:, j * SSM_WIDTH:(j + 1) * SSM_WIDTH])
        gate = jnp.dot(y.astype(BF16), wglu_ref[...], preferred_element_type=F32) + bglu_ref[...]
        res = y * jax.nn.sigmoid(gate)
        for h in range(SSM_WIDTH // LANES):
            tok_ref[h, pl.ds(j, rows, stride=lc), :] = res[:, h * LANES:(h + 1) * LANES]
    out = jnp.concatenate([tok_ref[h] for h in range(SSM_WIDTH // LANES)], axis=1)
    o_ref[...] = out.astype(BF16).reshape(nb, n_chunks * lc, SSM_WIDTH)


def _ssm(u2, bst, carry, toeplitz, a_pow, w_glu, b_glu, nb):
    bsz, n_chunks, _ = u2.shape
    lc = SSM_CHUNK
    seq = n_chunks * lc
    return pl.pallas_call(
        functools.partial(_ssm_body, nb=nb, n_chunks=n_chunks),
        grid=(bsz // nb,),
        in_specs=[pl.BlockSpec((nb, n_chunks, lc * SSM_WIDTH), lambda b: (b, 0, 0)),
                  _const_spec(bst.shape), _const_spec(carry.shape), _const_spec(toeplitz.shape),
                  _const_spec(a_pow.shape),
                  _const_spec((SSM_WIDTH, SSM_WIDTH)), _const_spec((1, SSM_WIDTH))],
        out_specs=pl.BlockSpec((nb, seq, SSM_WIDTH), lambda b: (b, 0, 0)),
        out_shape=jax.ShapeDtypeStruct((bsz, seq, SSM_WIDTH), BF16),
        scratch_shapes=[pltpu.VMEM((nb * n_chunks, STATE_COLS), F32),
                        pltpu.VMEM((nb * n_chunks, lc * SSM_WIDTH), F32),
                        pltpu.VMEM((SSM_WIDTH // LANES, nb * seq, LANES), F32)],
        compiler_params=_params(1),
        name="ssm",
    )(u2, bst, carry, toeplitz, a_pow, w_glu.astype(BF16), b_glu.reshape(1, SSM_WIDTH))


def _tail_body(x_ref, oatt_ref, sg_ref, sa_ref, ss_ref,
               mod_ref, wpa_ref, wps_ref, wo_ref, gf_ref, wup_ref, wc_ref, bc_ref,
               wdn_ref, gfin_ref, out_ref, carry_ref):
    tm = x_ref.shape[0]

    @pl.when(pl.program_id(1) == 0)
    def _():
        carry_ref[...] = jnp.zeros_like(carry_ref)

    y_att = jnp.dot(oatt_ref[...], wpa_ref[...], preferred_element_type=F32)
    y_ssm = jnp.dot(sg_ref[...], wps_ref[...], preferred_element_type=F32)
    merged = sa_ref[...].astype(F32) * y_att + ss_ref[...].astype(F32) * y_ssm
    mix = jnp.dot(merged.astype(BF16), wo_ref[...], preferred_element_type=F32)
    h1 = x_ref[...] + mod_ref[2:3, :] * mix

    u = _rms_modulate(h1, gf_ref[...], mod_ref[3:4, :], mod_ref[4:5, :]).astype(BF16)
    a = jnp.dot(u, wup_ref[:, 0:D_FF], preferred_element_type=F32)
    val = jnp.dot(u, wup_ref[:, D_FF:], preferred_element_type=F32)

    row = lax.broadcasted_iota(jnp.int32, a.shape, 0)
    prev1 = carry_ref[SUBLANES - 1:SUBLANES, :]
    prev2 = carry_ref[SUBLANES - 2:SUBLANES - 1, :]
    a1 = jnp.where(row == 0, prev1, pltpu.roll(a, 1, 0))
    a2 = jnp.where(row == 0, prev2, jnp.where(row == 1, prev1, pltpu.roll(a, 2, 0)))
    carry_ref[...] = a[tm - SUBLANES:, :]
    conv = bc_ref[...] + wc_ref[0:1, :] * a + wc_ref[1:2, :] * a1 + wc_ref[2:3, :] * a2
    act = (conv * jax.nn.sigmoid(conv) * val).astype(BF16)
    ffn = jnp.dot(act, wdn_ref[...], preferred_element_type=F32)
    h2 = h1 + mod_ref[5:6, :] * ffn

    ms = jnp.mean(h2 * h2, axis=-1, keepdims=True)
    out_ref[...] = (h2 * lax.rsqrt(ms + EPS)) * gfin_ref[...]


def _tail(x, o_att, sglu, sig_att, sig_ssm, mod3, w_proj_att, w_proj_ssm, w_out,
          g_ffn, w_up, w_conv, b_conv, w_down, g_final, tm):
    bsz, seq, d = x.shape

    def tok(width):
        return pl.BlockSpec((None, tm, width), lambda b, s: (b, s, 0))

    in_specs = [tok(d), tok(ATT_WIDTH), tok(SSM_WIDTH), tok(d), tok(d),
                pl.BlockSpec((None, 6, d), lambda b, s: (b, 0, 0)),
                _const_spec((ATT_WIDTH, d)), _const_spec((SSM_WIDTH, d)), _const_spec((d, d)),
                _const_spec((1, d)), _const_spec((d, 2 * D_FF)), _const_spec((CONV_W, D_FF)),
                _const_spec((1, D_FF)), _const_spec((D_FF, d)), _const_spec((1, d))]
    return pl.pallas_call(
        _tail_body,
        grid=(bsz, seq // tm),
        in_specs=in_specs,
        out_specs=tok(d),
        out_shape=jax.ShapeDtypeStruct((bsz, seq, d), F32),
        scratch_shapes=[pltpu.VMEM((SUBLANES, D_FF), F32)],
        compiler_params=_params(2),
        name="tail",
    )(x, o_att, sglu, sig_att, sig_ssm, mod3,
      w_proj_att.astype(BF16), w_proj_ssm.astype(BF16), w_out.astype(BF16),
      g_ffn.reshape(1, d), w_up.astype(BF16), w_conv, b_conv.reshape(1, D_FF),
      w_down.astype(BF16), g_final.reshape(1, d))


def kernel(x, c, w_ada, b_ada, g_mix, w_in, b_gate, a_re, a_im, log_dt, b_re, b_im, c_re, c_im,
           d_skip, w_glu, b_glu, w_proj_att, w_proj_ssm, w_out, g_ffn, w_up, w_conv, b_conv,
           w_down, g_final):
    depth = w_ada.shape[0]
    assert depth == 1, "the final RMSNorm is fused into the single layer's tail kernel"
    bsz, seq, d = x.shape
    l = 0
    mod3 = _ada(c, w_ada[l], b_ada[l]).reshape(bsz, 6, d)
    q, k, v, us, sig_att, sig_ssm = _inproj(x, mod3, g_mix[l], w_in[l], b_gate[l], tm=1024)

    o_att = _attention(q, k, v)

    ssm_mats = _ssm_matrices(a_re[l], a_im[l], log_dt[l], b_re[l], b_im[l], c_re[l], c_im[l],
                             d_skip[l])
    sglu = _ssm(us, *ssm_mats, w_glu[l], b_glu[l], nb=2)

    return _tail(x, o_att, sglu, sig_att, sig_ssm, mod3, w_proj_att[l], w_proj_ssm[l],
                 w_out[l], g_ffn[l], w_up[l], w_conv[l], b_conv[l], w_down[l], g_final, tm=512)
```

```python
import functools
import math

import jax
import jax.numpy as jnp
import numpy as np
from jax import lax
from jax.experimental import pallas as pl
from jax.experimental.pallas import tpu as pltpu

D_MODEL = 1024
N_HEADS = 8
HEAD_DIM = 64
ATT_WIDTH = N_HEADS * HEAD_DIM
PATTERNS = ((128, 1), (512, 4), (2048, 16))
SSM_GROUPS = 16
SSM_GROUP_CH = 16
SSM_WIDTH = SSM_GROUPS * SSM_GROUP_CH
SSM_STATE = 64
D_FF = 2048
CONV_W = 3
EPS = 1e-6
NEG_INF = -1e30
LOG2E = math.log2(math.e)

LANES = 128
SUBLANES = 8
VMEM_LIMIT_BYTES = 56 * 1024 * 1024

ATT_BLOCK = 128
SSM_CHUNK = 8
STATE_COLS = 2 * SSM_GROUPS * SSM_STATE

BF16 = jnp.bfloat16
F32 = jnp.float32


def _const_spec(shape):
    zeros = (0,) * len(shape)
    return pl.BlockSpec(shape, lambda *_: zeros, pipeline_mode=pl.Buffered(1))


def _params(n_axes):
    return pltpu.CompilerParams(
        dimension_semantics=("arbitrary",) * n_axes,
        vmem_limit_bytes=VMEM_LIMIT_BYTES)


def _ada_body(c_ref, w_ref, b_ref, o_ref):
    c = c_ref[...]
    act = (c * jax.nn.sigmoid(c)).astype(BF16)
    o_ref[...] = jnp.dot(act, w_ref[...], preferred_element_type=F32) + b_ref[...]


def _ada(c, w_ada, b_ada):
    bsz = c.shape[0]
    n_out = w_ada.shape[1]
    tn = 1536
    return pl.pallas_call(
        _ada_body,
        grid=(n_out // tn,),
        in_specs=[_const_spec((bsz, D_MODEL)),
                  pl.BlockSpec((D_MODEL, tn), lambda j: (0, j)),
                  pl.BlockSpec((1, tn), lambda j: (0, j))],
        out_specs=pl.BlockSpec((bsz, tn), lambda j: (0, j)),
        out_shape=jax.ShapeDtypeStruct((bsz, n_out), F32),
        compiler_params=_params(1),
        name="ada",
    )(c, w_ada.astype(BF16), b_ada.reshape(1, n_out))


def _rms_modulate(x, gain, shift, scale):
    ms = jnp.mean(x * x, axis=-1, keepdims=True)
    return (x * lax.rsqrt(ms + EPS)) * (gain * (1.0 + scale)) + shift


def _inproj_body(x_ref, mod_ref, g_ref, w_ref, bg_ref,
                 q_ref, k_ref, v_ref, us_ref, ug_ref, sa_ref, ss_ref, us_scr):
    u = _rms_modulate(x_ref[...], g_ref[...], mod_ref[0:1, :], mod_ref[1:2, :]).astype(BF16)

    def proj(lo, hi):
        return jnp.dot(u, w_ref[:, lo:hi], preferred_element_type=F32)

    a = ATT_WIDTH
    q_ref[...] = (proj(0, a) * (HEAD_DIM ** -0.5)).astype(BF16)
    k_ref[...] = proj(a, 2 * a).astype(BF16)
    v_ref[...] = proj(2 * a, 3 * a).astype(BF16)
    o = 3 * a
    us = proj(o, o + SSM_WIDTH)
    n_tiles = SSM_WIDTH // LANES
    for h in range(n_tiles):
        us_scr[h] = us[:, h * LANES:(h + 1) * LANES]
    n_rows = us_scr.shape[1] // SSM_CHUNK
    pieces = [[us_scr[h, pl.ds(s, n_rows, stride=SSM_CHUNK), :] for h in range(n_tiles)]
              for s in range(SSM_CHUNK)]
    us_ref[...] = jnp.concatenate(
        [pieces[s][h] for s in range(SSM_CHUNK) for h in range(n_tiles)], axis=1).astype(BF16)
    per_tile = LANES // SSM_GROUP_CH
    ug_ref[...] = jnp.concatenate(
        [pieces[s][g // per_tile][:, (g % per_tile) * SSM_GROUP_CH:(g % per_tile + 1) * SSM_GROUP_CH]
         for g in range(SSM_GROUPS) for s in range(SSM_CHUNK)], axis=1).astype(BF16)
    o += SSM_WIDTH
    sa_ref[...] = jax.nn.sigmoid(proj(o, o + D_MODEL) + bg_ref[:, 0:D_MODEL]).astype(BF16)
    o += D_MODEL
    ss_ref[...] = jax.nn.sigmoid(proj(o, o + D_MODEL) + bg_ref[:, D_MODEL:]).astype(BF16)


def _inproj(x, mod3, g_mix, w_in, b_gate, tm):
    bsz, seq, d = x.shape
    in_width = w_in.shape[1]

    def tok(width):
        return pl.BlockSpec((None, tm, width), lambda b, s: (b, s, 0))

    def out(width):
        return jax.ShapeDtypeStruct((bsz, seq, width), BF16)

    return pl.pallas_call(
        _inproj_body,
        grid=(bsz, seq // tm),
        in_specs=[tok(d),
                  pl.BlockSpec((None, 6, d), lambda b, s: (b, 0, 0)),
                  _const_spec((1, d)),
                  _const_spec((d, in_width)),
                  _const_spec((1, 2 * d))],
        out_specs=[tok(ATT_WIDTH), tok(ATT_WIDTH), tok(ATT_WIDTH),
                   pl.BlockSpec((None, tm // SSM_CHUNK, SSM_CHUNK * SSM_WIDTH), lambda b, s: (b, s, 0)),
                   pl.BlockSpec((None, tm // SSM_CHUNK, SSM_CHUNK * SSM_WIDTH), lambda b, s: (b, s, 0)),
                   tok(d), tok(d)],
        out_shape=[out(ATT_WIDTH), out(ATT_WIDTH), out(ATT_WIDTH),
                   jax.ShapeDtypeStruct((bsz, seq // SSM_CHUNK, SSM_CHUNK * SSM_WIDTH), BF16),
                   jax.ShapeDtypeStruct((bsz, seq // SSM_CHUNK, SSM_CHUNK * SSM_WIDTH), BF16),
                   out(d), out(d)],
        scratch_shapes=[pltpu.VMEM((SSM_WIDTH // LANES, tm, LANES), F32)],
        compiler_params=_params(2),
        name="inproj",
    )(x, mod3, g_mix.reshape(1, d), w_in.astype(BF16), b_gate.reshape(1, 2 * d))


def _alibi_distance_table(dilation, kw):
    w = ATT_BLOCK
    j = np.arange(kw)[:, None]
    a = np.arange(w)[None, :]
    tabs = []
    for sel in range(2):
        dist = (sel * w + a - j).astype(np.float32)
        valid = (dist >= 0) & (dist <= w)
        tabs.append(np.where(valid, -dilation * dist, NEG_INF))
    return np.stack(tabs).astype(np.float32)


def _alibi_slope_table():
    w = ATT_BLOCK
    tabs = np.zeros((N_HEADS // 2, 2 * w, w), np.float32)
    for h in range(N_HEADS):
        slope = 2.0 ** (-8.0 * (h + 1) / N_HEADS)
        tabs[h // 2, (h % 2) * w:(h % 2 + 1) * w, :] = slope * np.eye(w, dtype=np.float32)
    return tabs


def _pair_block(qp, kp, vp, slopes, dist_t, low_half):
    w = ATT_BLOCK
    kw = kp.shape[0]
    zero = jnp.zeros((), BF16)
    qs = jnp.concatenate([jnp.where(low_half, qp, zero), jnp.where(low_half, zero, qp)], axis=0)
    lhs = jnp.concatenate([qs, slopes], axis=1)
    rhs = jnp.concatenate([kp, dist_t], axis=1)
    s2 = lax.dot_general(lhs, rhs, (((1,), (1,)), ((), ())), preferred_element_type=F32)
    v_ones = jnp.concatenate([vp, jnp.ones((kw, LANES), BF16)], axis=1)
    res, maxes = [], []
    for hh in range(2):
        s = s2[hh * w:(hh + 1) * w]
        m = jnp.max(s, axis=-1, keepdims=True)
        p = jnp.exp2(((s - m) * LOG2E).astype(BF16))
        res.append(jnp.dot(p, v_ones, preferred_element_type=F32))
        maxes.append(m)
    num = jnp.where(low_half, res[0][:, :LANES], res[1][:, :LANES])
    den = jnp.where(low_half, res[0][:, LANES:], res[1][:, LANES:])
    return num, den, jnp.where(low_half, maxes[0], maxes[1])


def _attn_body(q_ref, k_ref, v_ref, slope_ref, b1_ref, b4_ref, b16_ref, o_ref,
               f32_ref, sub4f_ref, sub4_ref, sub16_ref, num_ref, den_ref, max_ref, *, seq):
    w = ATT_BLOCK
    pair = pl.program_id(1)
    low_half = lax.broadcasted_iota(jnp.int32, (w, LANES), 1) < HEAD_DIM
    slopes = slope_ref[pair]
    srcs = (q_ref, k_ref, v_ref)
    n4, n16 = seq // 4, seq // 16
    for t in range(3):
        f32_ref[t] = srcs[t][...].astype(F32)
    for t in range(3):
        for r4 in range(4):
            sub4f_ref[t, r4 * n4:(r4 + 1) * n4, :] = f32_ref[t, pl.ds(r4, n4, stride=4), :]
    for t in range(3):
        sub4_ref[t] = sub4f_ref[t].astype(BF16)
        for r4 in range(4):
            for q4 in range(4):
                r16 = 4 * q4 + r4
                sub16_ref[t, r16 * n16:(r16 + 1) * n16, :] = (
                    sub4f_ref[t, pl.ds(r4 * n4 + q4, n16, stride=4), :].astype(BF16))

    def block(load, bias_ref, sub_len, r, i):
        base = r * sub_len
        kw = w if i == 0 else 2 * w
        krow = base + max(i - 1, 0) * w
        sel = min(i, 1)
        return _pair_block(load(0, base + i * w, w), load(1, krow, kw), load(2, krow, kw),
                           slopes, bias_ref[sel, 0:kw, :], low_half)

    dilated = ((0, 4, b4_ref, lambda t, row, size: sub4_ref[t, row:row + size, :]),
               (1, 16, b16_ref, lambda t, row, size: sub16_ref[t, row:row + size, :]))
    for pidx, dilation, bias_ref, load in dilated:
        sub_len = seq // dilation
        for r in range(dilation):
            for i in range(sub_len // w):
                num, den, m = block(load, bias_ref, sub_len, r, i)
                dst = pl.ds(dilation * w * i + r, w, stride=dilation)
                num_ref[pidx, dst, :] = num
                den_ref[pidx, dst, :] = den
                max_ref[pidx, dst, :] = m

    for i in range(seq // w):
        num1, den1, m1 = block(lambda t, row, size: srcs[t][row:row + size, :], b1_ref, seq, 0, i)
        rows = slice(i * w, (i + 1) * w)
        m2, m3 = max_ref[0, rows, :], max_ref[1, rows, :]
        mx = jnp.maximum(jnp.maximum(m1, m2), m3)
        e1, e2, e3 = jnp.exp(m1 - mx), jnp.exp(m2 - mx), jnp.exp(m3 - mx)
        num = e1 * num1 + e2 * num_ref[0, rows, :] + e3 * num_ref[1, rows, :]
        den = e1 * den1 + e2 * den_ref[0, rows, :] + e3 * den_ref[1, rows, :]
        o_ref[rows, :] = (num * (1.0 / den)).astype(BF16)


def _attention(q, k, v):
    bsz, seq, _ = q.shape
    w = ATT_BLOCK
    slopes = jnp.asarray(_alibi_slope_table(), BF16)
    biases = [jnp.asarray(_alibi_distance_table(d, min(2 * w, seq // d)), BF16) for _, d in PATTERNS]
    blk = pl.BlockSpec((None, seq, LANES), lambda b, p: (b, 0, p))
    return pl.pallas_call(
        functools.partial(_attn_body, seq=seq),
        grid=(bsz, N_HEADS // 2),
        in_specs=[blk, blk, blk, _const_spec(slopes.shape)] + [_const_spec(t.shape) for t in biases],
        out_specs=blk,
        out_shape=jax.ShapeDtypeStruct((bsz, seq, ATT_WIDTH), BF16),
        scratch_shapes=[pltpu.VMEM((3, seq, LANES), F32),
                        pltpu.VMEM((3, seq, LANES), F32),
                        pltpu.VMEM((3, seq, LANES), BF16),
                        pltpu.VMEM((3, seq, LANES), BF16),
                        pltpu.VMEM((2, seq, LANES), F32),
                        pltpu.VMEM((2, seq, LANES), F32),
                        pltpu.VMEM((2, seq, LANES), F32)],
        compiler_params=_params(2),
        name="attn",
    )(q, k, v, slopes, *biases)


def _placement(n_outer, n_inner):
    t = np.zeros((n_outer, n_inner, n_outer, SSM_GROUPS, n_inner), np.float32)
    for o in range(n_outer):
        for i in range(n_inner):
            t[o, i, o, :, i] = 1.0
    return t.reshape(n_outer * n_inner, n_outer * SSM_GROUPS * n_inner)


def _block_diagonal(val, n_outer, n_inner, row_inner):
    full = jnp.dot(val.astype(BF16), jnp.asarray(_placement(n_outer, n_inner)).astype(BF16),
                   preferred_element_type=F32)
    row_g = (lax.broadcasted_iota(jnp.int32, full.shape, 0) // row_inner) % SSM_GROUPS
    col_g = (lax.broadcasted_iota(jnp.int32, full.shape, 1) // n_inner) % SSM_GROUPS
    return jnp.where(row_g == col_g, full, 0.0).astype(BF16)


def _ssm_matrices(a_re, a_im, log_dt, b_re, b_im, c_re, c_im, d_skip):
    lc, g_n, n_st, ch = SSM_CHUNK, SSM_GROUPS, SSM_STATE, SSM_GROUP_CH
    hi = lax.Precision.HIGHEST
    lr, li = a_re.astype(F32), a_im.astype(F32)
    dt = jnp.exp(log_dt.astype(F32))[:, None]
    mag = jnp.exp(lr * dt)
    ang = li * dt
    ab_re, ab_im = mag * jnp.cos(ang), mag * jnp.sin(ang)
    nr, ni = ab_re - 1.0, ab_im
    den = lr * lr + li * li
    f_re = (nr * lr + ni * li) / den
    f_im = (ni * lr - nr * li) / den
    bb_re = f_re[..., None] * b_re - f_im[..., None] * b_im
    bb_im = f_re[..., None] * b_im + f_im[..., None] * b_re

    kk = jnp.arange(lc + 1, dtype=F32)[:, None, None]
    pw_mag = jnp.exp(kk * (lr * dt)[None])
    pw_re = pw_mag * jnp.cos(kk * ang[None])
    pw_im = pw_mag * jnp.sin(kk * ang[None])

    cp_re = c_re[None] * pw_re[:, :, None, :] - c_im[None] * pw_im[:, :, None, :]
    cp_im = c_re[None] * pw_im[:, :, None, :] + c_im[None] * pw_re[:, :, None, :]

    taps = (jnp.einsum('kgcn,gni->kgci', cp_re[:lc], bb_re, precision=hi)
            - jnp.einsum('kgcn,gni->kgci', cp_im[:lc], bb_im, precision=hi))
    skip = d_skip.reshape(g_n, ch)
    taps = taps.at[0].add(skip[:, :, None] * jnp.eye(ch, dtype=F32)[None])
    lag = np.arange(lc)[None, :] - np.arange(lc)[:, None]
    tz = taps[np.maximum(lag, 0)] * jnp.asarray(lag >= 0, F32)[:, :, None, None, None]
    tz = tz.transpose(0, 2, 4, 1, 3).reshape(lc * SSM_WIDTH, lc * ch)
    toeplitz = _block_diagonal(tz, lc, ch, ch)

    cv = jnp.stack([cp_re[1:], -cp_im[1:]], axis=0)
    cv = cv.transpose(0, 2, 4, 1, 3).reshape(STATE_COLS, lc * ch)
    carry = _block_diagonal(cv, lc, ch, n_st)

    rp_re, rp_im = pw_re[:lc][::-1], pw_im[:lc][::-1]
    bp_re = rp_re[..., None] * bb_re[None] - rp_im[..., None] * bb_im[None]
    bp_im = rp_re[..., None] * bb_im[None] + rp_im[..., None] * bb_re[None]
    bv = jnp.stack([bp_re, bp_im], axis=0).reshape(2, lc, g_n // 2, 2, n_st, ch)
    bst = jnp.einsum('pskinc,ij->kiscpjn', bv, jnp.eye(2, dtype=F32))
    bst = bst.reshape(g_n // 2, 2 * lc * ch, 4 * n_st).astype(BF16)

    half = g_n * n_st
    a_pow = jnp.concatenate([pw_re[lc].reshape(1, half), pw_im[lc].reshape(1, half)], axis=1)
    return bst, carry, toeplitz, a_pow


def _ssm_body(u_ref, ug_ref, bst_ref, wc_ref, wt_ref, apow_ref, wglu_ref, bglu_ref, o_ref,
              st_ref, y_ref, tok_ref, *, nb, n_chunks):
    lc = SSM_CHUNK
    half = STATE_COLS // 2
    rows = nb * n_chunks
    u2 = u_ref[...].reshape(rows, lc * SSM_WIDTH)
    ug = ug_ref[...].reshape(rows, lc * SSM_WIDTH)
    pw = 2 * SSM_STATE
    for k in range(SSM_GROUPS // 2):
        res = jnp.dot(ug[:, 2 * k * pw:2 * (k + 1) * pw], bst_ref[k], preferred_element_type=F32)
        st_ref[:, k * pw:(k + 1) * pw] = res[:, :pw]
        st_ref[:, half + k * pw:half + (k + 1) * pw] = res[:, pw:]

    a_r = apow_ref[:, 0:half]
    a_i = apow_ref[:, half:]

    def step(c, carry):
        new = []
        for b in range(nb):
            x_r, x_i = carry[2 * b], carry[2 * b + 1]
            row = pl.ds(b * n_chunks + c, 1)
            s_r = st_ref[row, 0:half]
            s_i = st_ref[row, half:]
            st_ref[row, 0:half] = x_r
            st_ref[row, half:] = x_i
            new.append(a_r * x_r - a_i * x_i + s_r)
            new.append(a_r * x_i + a_i * x_r + s_i)
        return tuple(new)

    init = tuple(jnp.zeros((1, half), F32) for _ in range(2 * nb))
    lax.fori_loop(0, n_chunks, step, init)

    x_in = st_ref[...].astype(BF16)
    hw = lc * SSM_WIDTH // 2
    for half_idx in range(2):
        cols = slice(half_idx * hw, (half_idx + 1) * hw)
        kdim = (half_idx + 1) * hw
        y_ref[:, cols] = (
            jnp.dot(x_in, wc_ref[:, cols], preferred_element_type=F32)
            + jnp.dot(u2[:, :kdim], wt_ref[0:kdim, cols], preferred_element_type=F32))
    for j in range(lc):
        y = jax.nn.gelu(y_ref[:, j * SSM_WIDTH:(j + 1) * SSM_WIDTH])
        gate = jnp.dot(y.astype(BF16), wglu_ref[...], preferred_element_type=F32) + bglu_ref[...]
        res = y * jax.nn.sigmoid(gate)
        for h in range(SSM_WIDTH // LANES):
            tok_ref[h, pl.ds(j, rows, stride=lc), :] = res[:, h * LANES:(h + 1) * LANES]
    out = jnp.concatenate([tok_ref[h] for h in range(SSM_WIDTH // LANES)], axis=1)
    o_ref[...] = out.astype(BF16).reshape(nb, n_chunks * lc, SSM_WIDTH)


def _ssm(u2, ug, bst, carry, toeplitz, a_pow, w_glu, b_glu, nb):
    bsz, n_chunks, _ = u2.shape
    lc = SSM_CHUNK
    seq = n_chunks * lc
    return pl.pallas_call(
        functools.partial(_ssm_body, nb=nb, n_chunks=n_chunks),
        grid=(bsz // nb,),
        in_specs=[pl.BlockSpec((nb, n_chunks, lc * SSM_WIDTH), lambda b: (b, 0, 0)),
                  pl.BlockSpec((nb, n_chunks, lc * SSM_WIDTH), lambda b: (b, 0, 0)),
                  _const_spec(bst.shape), _const_spec(carry.shape), _const_spec(toeplitz.shape),
                  _const_spec(a_pow.shape),
                  _const_spec((SSM_WIDTH, SSM_WIDTH)), _const_spec((1, SSM_WIDTH))],
        out_specs=pl.BlockSpec((nb, seq, SSM_WIDTH), lambda b: (b, 0, 0)),
        out_shape=jax.ShapeDtypeStruct((bsz, seq, SSM_WIDTH), BF16),
        scratch_shapes=[pltpu.VMEM((nb * n_chunks, STATE_COLS), F32),
                        pltpu.VMEM((nb * n_chunks, lc * SSM_WIDTH), F32),
                        pltpu.VMEM((SSM_WIDTH // LANES, nb * seq, LANES), F32)],
        compiler_params=_params(1),
        name="ssm",
    )(u2, ug, bst, carry, toeplitz, a_pow, w_glu.astype(BF16), b_glu.reshape(1, SSM_WIDTH))


def _tail_body(x_ref, oatt_ref, sg_ref, sa_ref, ss_ref,
               mod_ref, wpa_ref, wps_ref, wo_ref, gf_ref, wup_ref, wc_ref, bc_ref,
               wdn_ref, gfin_ref, out_ref, carry_ref):
    tm = x_ref.shape[0]

    @pl.when(pl.program_id(1) == 0)
    def _():
        carry_ref[...] = jnp.zeros_like(carry_ref)

    y_att = jnp.dot(oatt_ref[...], wpa_ref[...], preferred_element_type=F32)
    y_ssm = jnp.dot(sg_ref[...], wps_ref[...], preferred_element_type=F32)
    merged = sa_ref[...].astype(F32) * y_att + ss_ref[...].astype(F32) * y_ssm
    mix = jnp.dot(merged.astype(BF16), wo_ref[...], preferred_element_type=F32)
    h1 = x_ref[...] + mod_ref[2:3, :] * mix

    u = _rms_modulate(h1, gf_ref[...], mod_ref[3:4, :], mod_ref[4:5, :]).astype(BF16)
    a = jnp.dot(u, wup_ref[:, 0:D_FF], preferred_element_type=F32)
    val = jnp.dot(u, wup_ref[:, D_FF:], preferred_element_type=F32)

    row = lax.broadcasted_iota(jnp.int32, a.shape, 0)
    prev1 = carry_ref[SUBLANES - 1:SUBLANES, :]
    prev2 = carry_ref[SUBLANES - 2:SUBLANES - 1, :]
    a1 = jnp.where(row == 0, prev1, pltpu.roll(a, 1, 0))
    a2 = jnp.where(row == 0, prev2, jnp.where(row == 1, prev1, pltpu.roll(a, 2, 0)))
    carry_ref[...] = a[tm - SUBLANES:, :]
    conv = bc_ref[...] + wc_ref[0:1, :] * a + wc_ref[1:2, :] * a1 + wc_ref[2:3, :] * a2
    act = (conv * jax.nn.sigmoid(conv) * val).astype(BF16)
    ffn = jnp.dot(act, wdn_ref[...], preferred_element_type=F32)
    h2 = h1 + mod_ref[5:6, :] * ffn

    ms = jnp.mean(h2 * h2, axis=-1, keepdims=True)
    out_ref[...] = (h2 * lax.rsqrt(ms + EPS)) * gfin_ref[...]


def _tail(x, o_att, sglu, sig_att, sig_ssm, mod3, w_proj_att, w_proj_ssm, w_out,
          g_ffn, w_up, w_conv, b_conv, w_down, g_final, tm):
    bsz, seq, d = x.shape

    def tok(width):
        return pl.BlockSpec((None, tm, width), lambda b, s: (b, s, 0))

    in_specs = [tok(d), tok(ATT_WIDTH), tok(SSM_WIDTH), tok(d), tok(d),
                pl.BlockSpec((None, 6, d), lambda b, s: (b, 0, 0)),
                _const_spec((ATT_WIDTH, d)), _const_spec((SSM_WIDTH, d)), _const_spec((d, d)),
                _const_spec((1, d)), _const_spec((d, 2 * D_FF)), _const_spec((CONV_W, D_FF)),
                _const_spec((1, D_FF)), _const_spec((D_FF, d)), _const_spec((1, d))]
    return pl.pallas_call(
        _tail_body,
        grid=(bsz, seq // tm),
        in_specs=in_specs,
        out_specs=tok(d),
        out_shape=jax.ShapeDtypeStruct((bsz, seq, d), F32),
        scratch_shapes=[pltpu.VMEM((SUBLANES, D_FF), F32)],
        compiler_params=_params(2),
        name="tail",
    )(x, o_att, sglu, sig_att, sig_ssm, mod3,
      w_proj_att.astype(BF16), w_proj_ssm.astype(BF16), w_out.astype(BF16),
      g_ffn.reshape(1, d), w_up.astype(BF16), w_conv, b_conv.reshape(1, D_FF),
      w_down.astype(BF16), g_final.reshape(1, d))


def kernel(x, c, w_ada, b_ada, g_mix, w_in, b_gate, a_re, a_im, log_dt, b_re, b_im, c_re, c_im,
           d_skip, w_glu, b_glu, w_proj_att, w_proj_ssm, w_out, g_ffn, w_up, w_conv, b_conv,
           w_down, g_final):
    depth = w_ada.shape[0]
    assert depth == 1, "the final RMSNorm is fused into the single layer's tail kernel"
    bsz, seq, d = x.shape
    l = 0
    mod3 = _ada(c, w_ada[l], b_ada[l]).reshape(bsz, 6, d)
    q, k, v, us, ug, sig_att, sig_ssm = _inproj(x, mod3, g_mix[l], w_in[l], b_gate[l], tm=1024)

    o_att = _attention(q, k, v)

    ssm_mats = _ssm_matrices(a_re[l], a_im[l], log_dt[l], b_re[l], b_im[l], c_re[l], c_im[l],
                             d_skip[l])
    sglu = _ssm(us, ug, *ssm_mats, w_glu[l], b_glu[l], nb=2)

    return _tail(x, o_att, sglu, sig_att, sig_ssm, mod3, w_proj_att[l], w_proj_ssm[l],
                 w_out[l], g_ffn[l], w_up[l], w_conv[l], b_conv[l], w_down[l], g_final, tm=512)
```

```python
import functools
import math

import jax
import jax.numpy as jnp
import numpy as np
from jax import lax
from jax.experimental import pallas as pl
from jax.experimental.pallas import tpu as pltpu

D_MODEL = 1024
N_HEADS = 8
HEAD_DIM = 64
ATT_WIDTH = N_HEADS * HEAD_DIM
PATTERNS = ((128, 1), (512, 4), (2048, 16))
SSM_GROUPS = 16
SSM_GROUP_CH = 16
SSM_WIDTH = SSM_GROUPS * SSM_GROUP_CH
SSM_STATE = 64
D_FF = 2048
CONV_W = 3
EPS = 1e-6
NEG_INF = -1e30
LOG2E = math.log2(math.e)

LANES = 128
SUBLANES = 8
VMEM_BYTES = 64 * 1024 * 1024
VMEM_LIMIT_BYTES = 56 * 1024 * 1024
TAIL_VMEM_LIMIT_BYTES = VMEM_BYTES - 4 * 1024 * 1024

ATT_BLOCK = 128
ATT_PAIRS_PER_STEP = 2
SSM_CHUNK = 8
STATE_COLS = 2 * SSM_GROUPS * SSM_STATE

BF16 = jnp.bfloat16
F32 = jnp.float32


def _const_spec(shape):
    zeros = (0,) * len(shape)
    return pl.BlockSpec(shape, lambda *_: zeros, pipeline_mode=pl.Buffered(1))


def _params(n_axes, vmem_limit_bytes=VMEM_LIMIT_BYTES):
    return pltpu.CompilerParams(
        dimension_semantics=("arbitrary",) * n_axes,
        vmem_limit_bytes=vmem_limit_bytes)


def _ada_body(c_ref, w_ref, b_ref, o_ref):
    c = c_ref[...]
    act = (c * jax.nn.sigmoid(c)).astype(BF16)
    o_ref[...] = jnp.dot(act, w_ref[...], preferred_element_type=F32) + b_ref[...]


def _ada(c, w_ada, b_ada):
    bsz = c.shape[0]
    n_out = w_ada.shape[1]
    tn = 1536
    return pl.pallas_call(
        _ada_body,
        grid=(n_out // tn,),
        in_specs=[_const_spec((bsz, D_MODEL)),
                  pl.BlockSpec((D_MODEL, tn), lambda j: (0, j)),
                  pl.BlockSpec((1, tn), lambda j: (0, j))],
        out_specs=pl.BlockSpec((bsz, tn), lambda j: (0, j)),
        out_shape=jax.ShapeDtypeStruct((bsz, n_out), F32),
        compiler_params=_params(1),
        name="ada",
    )(c, w_ada.astype(BF16), b_ada.reshape(1, n_out))


def _rms_modulate(x, gain, shift, scale):
    ms = jnp.mean(x * x, axis=-1, keepdims=True)
    return (x * lax.rsqrt(ms + EPS)) * (gain * (1.0 + scale)) + shift


def _inproj_body(x_ref, mod_ref, g_ref, w_ref, bg_ref,
                 q_ref, k_ref, v_ref, us_ref, ug_ref, sa_ref, ss_ref, us_scr):
    u = _rms_modulate(x_ref[...], g_ref[...], mod_ref[0:1, :], mod_ref[1:2, :]).astype(BF16)

    def proj(lo, hi):
        return jnp.dot(u, w_ref[:, lo:hi], preferred_element_type=F32)

    a = ATT_WIDTH
    q_ref[...] = (proj(0, a) * (HEAD_DIM ** -0.5)).astype(BF16)
    k_ref[...] = proj(a, 2 * a).astype(BF16)
    v_ref[...] = proj(2 * a, 3 * a).astype(BF16)
    o = 3 * a
    us = proj(o, o + SSM_WIDTH)
    n_tiles = SSM_WIDTH // LANES
    for h in range(n_tiles):
        us_scr[h] = us[:, h * LANES:(h + 1) * LANES]
    n_rows = us_scr.shape[1] // SSM_CHUNK
    pieces = [[us_scr[h, pl.ds(s, n_rows, stride=SSM_CHUNK), :] for h in range(n_tiles)]
              for s in range(SSM_CHUNK)]
    us_ref[...] = jnp.concatenate(
        [pieces[s][h] for s in range(SSM_CHUNK) for h in range(n_tiles)], axis=1).astype(BF16)
    per_tile = LANES // SSM_GROUP_CH
    ug_ref[...] = jnp.concatenate(
        [pieces[s][g // per_tile][:, (g % per_tile) * SSM_GROUP_CH:(g % per_tile + 1) * SSM_GROUP_CH]
         for g in range(SSM_GROUPS) for s in range(SSM_CHUNK)], axis=1).astype(BF16)
    o += SSM_WIDTH
    sa_ref[...] = jax.nn.sigmoid(proj(o, o + D_MODEL) + bg_ref[:, 0:D_MODEL]).astype(BF16)
    o += D_MODEL
    ss_ref[...] = jax.nn.sigmoid(proj(o, o + D_MODEL) + bg_ref[:, D_MODEL:]).astype(BF16)


def _inproj(x, mod3, g_mix, w_in, b_gate, tm):
    bsz, seq, d = x.shape
    in_width = w_in.shape[1]

    def tok(width):
        return pl.BlockSpec((None, tm, width), lambda b, s: (b, s, 0))

    def out(width):
        return jax.ShapeDtypeStruct((bsz, seq, width), BF16)

    return pl.pallas_call(
        _inproj_body,
        grid=(bsz, seq // tm),
        in_specs=[tok(d),
                  pl.BlockSpec((None, 6, d), lambda b, s: (b, 0, 0)),
                  _const_spec((1, d)),
                  _const_spec((d, in_width)),
                  _const_spec((1, 2 * d))],
        out_specs=[tok(ATT_WIDTH), tok(ATT_WIDTH), tok(ATT_WIDTH),
                   pl.BlockSpec((None, tm // SSM_CHUNK, SSM_CHUNK * SSM_WIDTH), lambda b, s: (b, s, 0)),
                   pl.BlockSpec((None, tm // SSM_CHUNK, SSM_CHUNK * SSM_WIDTH), lambda b, s: (b, s, 0)),
                   tok(d), tok(d)],
        out_shape=[out(ATT_WIDTH), out(ATT_WIDTH), out(ATT_WIDTH),
                   jax.ShapeDtypeStruct((bsz, seq // SSM_CHUNK, SSM_CHUNK * SSM_WIDTH), BF16),
                   jax.ShapeDtypeStruct((bsz, seq // SSM_CHUNK, SSM_CHUNK * SSM_WIDTH), BF16),
                   out(d), out(d)],
        scratch_shapes=[pltpu.VMEM((SSM_WIDTH // LANES, tm, LANES), F32)],
        compiler_params=_params(2),
        name="inproj",
    )(x, mod3, g_mix.reshape(1, d), w_in.astype(BF16), b_gate.reshape(1, 2 * d))


def _alibi_distance_table(dilation, kw):
    w = ATT_BLOCK
    j = np.arange(kw)[:, None]
    a = np.arange(w)[None, :]
    tabs = []
    for sel in range(2):
        dist = (sel * w + a - j).astype(np.float32)
        valid = (dist >= 0) & (dist <= w)
        tabs.append(np.where(valid, -dilation * dist, NEG_INF))
    return np.stack(tabs).astype(np.float32)


def _alibi_slope_table():
    w = ATT_BLOCK
    tabs = np.zeros((N_HEADS // 2, 2 * w, w), np.float32)
    for h in range(N_HEADS):
        slope = 2.0 ** (-8.0 * (h + 1) / N_HEADS)
        tabs[h // 2, (h % 2) * w:(h % 2 + 1) * w, :] = slope * np.eye(w, dtype=np.float32)
    return tabs


def _pair_block(qp, kp, vp, slopes, dist_t, low_half):
    w = ATT_BLOCK
    kw = kp.shape[0]
    zero = jnp.zeros((), BF16)
    qs = jnp.concatenate([jnp.where(low_half, qp, zero), jnp.where(low_half, zero, qp)], axis=0)
    lhs = jnp.concatenate([qs, slopes], axis=1)
    rhs = jnp.concatenate([kp, dist_t], axis=1)
    s2 = lax.dot_general(lhs, rhs, (((1,), (1,)), ((), ())), preferred_element_type=F32)
    v_ones = jnp.concatenate([vp, jnp.ones((kw, LANES), BF16)], axis=1)
    res, maxes = [], []
    for hh in range(2):
        s = s2[hh * w:(hh + 1) * w]
        m = jnp.max(s, axis=-1, keepdims=True)
        p = jnp.exp2(((s - m) * LOG2E).astype(BF16))
        res.append(jnp.dot(p, v_ones, preferred_element_type=F32))
        maxes.append(m)
    num = jnp.where(low_half, res[0][:, :LANES], res[1][:, :LANES])
    den = jnp.where(low_half, res[0][:, LANES:], res[1][:, LANES:])
    return num, den, jnp.where(low_half, maxes[0], maxes[1])


def _attn_pair(q_ref, k_ref, v_ref, slope_ref, b1_ref, b4_ref, b16_ref, o_ref,
               f32_ref, sub4f_ref, sub4_ref, sub16_ref, num_ref, den_ref, max_ref, *, seq, pair, lanes):
    w = ATT_BLOCK
    low_half = lax.broadcasted_iota(jnp.int32, (w, LANES), 1) < HEAD_DIM
    slopes = slope_ref[pair]
    srcs = (q_ref, k_ref, v_ref)
    n4, n16 = seq // 4, seq // 16
    for t in range(3):
        f32_ref[t] = srcs[t][:, lanes].astype(F32)
    for t in range(3):
        for r4 in range(4):
            sub4f_ref[t, r4 * n4:(r4 + 1) * n4, :] = f32_ref[t, pl.ds(r4, n4, stride=4), :]
    for t in range(3):
        sub4_ref[t] = sub4f_ref[t].astype(BF16)
        for r4 in range(4):
            for q4 in range(4):
                r16 = 4 * q4 + r4
                sub16_ref[t, r16 * n16:(r16 + 1) * n16, :] = (
                    sub4f_ref[t, pl.ds(r4 * n4 + q4, n16, stride=4), :].astype(BF16))

    def block(load, bias_ref, sub_len, r, i):
        base = r * sub_len
        kw = w if i == 0 else 2 * w
        krow = base + max(i - 1, 0) * w
        sel = min(i, 1)
        return _pair_block(load(0, base + i * w, w), load(1, krow, kw), load(2, krow, kw),
                           slopes, bias_ref[sel, 0:kw, :], low_half)

    dilated = ((0, 4, b4_ref, lambda t, row, size: sub4_ref[t, row:row + size, :]),
               (1, 16, b16_ref, lambda t, row, size: sub16_ref[t, row:row + size, :]))
    for pidx, dilation, bias_ref, load in dilated:
        sub_len = seq // dilation
        for r in range(dilation):
            for i in range(sub_len // w):
                num, den, m = block(load, bias_ref, sub_len, r, i)
                dst = pl.ds(dilation * w * i + r, w, stride=dilation)
                num_ref[pidx, dst, :] = num
                den_ref[pidx, dst, :] = den
                max_ref[pidx, dst, :] = m

    for i in range(seq // w):
        num1, den1, m1 = block(lambda t, row, size: srcs[t][row:row + size, lanes], b1_ref, seq, 0, i)
        rows = slice(i * w, (i + 1) * w)
        m2, m3 = max_ref[0, rows, :], max_ref[1, rows, :]
        mx = jnp.maximum(jnp.maximum(m1, m2), m3)
        e1, e2, e3 = jnp.exp(m1 - mx), jnp.exp(m2 - mx), jnp.exp(m3 - mx)
        num = e1 * num1 + e2 * num_ref[0, rows, :] + e3 * num_ref[1, rows, :]
        den = e1 * den1 + e2 * den_ref[0, rows, :] + e3 * den_ref[1, rows, :]
        o_ref[rows, lanes] = (num * (1.0 / den)).astype(BF16)


def _attn_body(*refs, seq, pairs_per_step):
    for j in range(pairs_per_step):
        _attn_pair(*refs, seq=seq, pair=pl.program_id(1) * pairs_per_step + j,
                   lanes=slice(j * LANES, (j + 1) * LANES))


def _attention(q, k, v):
    bsz, seq, _ = q.shape
    w = ATT_BLOCK
    slopes = jnp.asarray(_alibi_slope_table(), BF16)
    biases = [jnp.asarray(_alibi_distance_table(d, min(2 * w, seq // d)), BF16) for _, d in PATTERNS]
    pps = ATT_PAIRS_PER_STEP
    blk = pl.BlockSpec((None, seq, pps * LANES), lambda b, p: (b, 0, p))
    return pl.pallas_call(
        functools.partial(_attn_body, seq=seq, pairs_per_step=pps),
        grid=(bsz, N_HEADS // 2 // pps),
        in_specs=[blk, blk, blk, _const_spec(slopes.shape)] + [_const_spec(t.shape) for t in biases],
        out_specs=blk,
        out_shape=jax.ShapeDtypeStruct((bsz, seq, ATT_WIDTH), BF16),
        scratch_shapes=[pltpu.VMEM((3, seq, LANES), F32),
                        pltpu.VMEM((3, seq, LANES), F32),
                        pltpu.VMEM((3, seq, LANES), BF16),
                        pltpu.VMEM((3, seq, LANES), BF16),
                        pltpu.VMEM((2, seq, LANES), F32),
                        pltpu.VMEM((2, seq, LANES), F32),
                        pltpu.VMEM((2, seq, LANES), F32)],
        compiler_params=_params(2),
        name="attn",
    )(q, k, v, slopes, *biases)


def _placement(n_outer, n_inner):
    t = np.zeros((n_outer, n_inner, n_outer, SSM_GROUPS, n_inner), np.float32)
    for o in range(n_outer):
        for i in range(n_inner):
            t[o, i, o, :, i] = 1.0
    return t.reshape(n_outer * n_inner, n_outer * SSM_GROUPS * n_inner)


def _block_diagonal(val, n_outer, n_inner, row_inner):
    full = jnp.dot(val.astype(BF16), jnp.asarray(_placement(n_outer, n_inner)).astype(BF16),
                   preferred_element_type=F32)
    row_g = (lax.broadcasted_iota(jnp.int32, full.shape, 0) // row_inner) % SSM_GROUPS
    col_g = (lax.broadcasted_iota(jnp.int32, full.shape, 1) // n_inner) % SSM_GROUPS
    return jnp.where(row_g == col_g, full, 0.0).astype(BF16)


def _ssm_matrices(a_re, a_im, log_dt, b_re, b_im, c_re, c_im, d_skip):
    lc, g_n, n_st, ch = SSM_CHUNK, SSM_GROUPS, SSM_STATE, SSM_GROUP_CH
    hi = lax.Precision.HIGHEST
    lr, li = a_re.astype(F32), a_im.astype(F32)
    dt = jnp.exp(log_dt.astype(F32))[:, None]
    mag = jnp.exp(lr * dt)
    ang = li * dt
    ab_re, ab_im = mag * jnp.cos(ang), mag * jnp.sin(ang)
    nr, ni = ab_re - 1.0, ab_im
    den = lr * lr + li * li
    f_re = (nr * lr + ni * li) / den
    f_im = (ni * lr - nr * li) / den
    bb_re = f_re[..., None] * b_re - f_im[..., None] * b_im
    bb_im = f_re[..., None] * b_im + f_im[..., None] * b_re

    kk = jnp.arange(lc + 1, dtype=F32)[:, None, None]
    pw_mag = jnp.exp(kk * (lr * dt)[None])
    pw_re = pw_mag * jnp.cos(kk * ang[None])
    pw_im = pw_mag * jnp.sin(kk * ang[None])

    cp_re = c_re[None] * pw_re[:, :, None, :] - c_im[None] * pw_im[:, :, None, :]
    cp_im = c_re[None] * pw_im[:, :, None, :] + c_im[None] * pw_re[:, :, None, :]

    taps = (jnp.einsum('kgcn,gni->kgci', cp_re[:lc], bb_re, precision=hi)
            - jnp.einsum('kgcn,gni->kgci', cp_im[:lc], bb_im, precision=hi))
    skip = d_skip.reshape(g_n, ch)
    taps = taps.at[0].add(skip[:, :, None] * jnp.eye(ch, dtype=F32)[None])
    lag = np.arange(lc)[None, :] - np.arange(lc)[:, None]
    tz = taps[np.maximum(lag, 0)] * jnp.asarray(lag >= 0, F32)[:, :, None, None, None]
    tz = tz.transpose(0, 2, 4, 1, 3).reshape(lc * SSM_WIDTH, lc * ch)
    toeplitz = _block_diagonal(tz, lc, ch, ch)

    cv = jnp.stack([cp_re[1:], -cp_im[1:]], axis=0)
    cv = cv.transpose(0, 2, 4, 1, 3).reshape(STATE_COLS, lc * ch)
    carry = _block_diagonal(cv, lc, ch, n_st)

    rp_re, rp_im = pw_re[:lc][::-1], pw_im[:lc][::-1]
    bp_re = rp_re[..., None] * bb_re[None] - rp_im[..., None] * bb_im[None]
    bp_im = rp_re[..., None] * bb_im[None] + rp_im[..., None] * bb_re[None]
    bv = jnp.stack([bp_re, bp_im], axis=0).reshape(2, lc, g_n // 2, 2, n_st, ch)
    bst = jnp.einsum('pskinc,ij->kiscpjn', bv, jnp.eye(2, dtype=F32))
    bst = bst.reshape(g_n // 2, 2 * lc * ch, 4 * n_st).astype(BF16)

    half = g_n * n_st
    a_pow = jnp.concatenate([pw_re[lc].reshape(1, half), pw_im[lc].reshape(1, half)], axis=1)
    return bst, carry, toeplitz, a_pow


def _ssm_body(u_ref, ug_ref, bst_ref, wc_ref, wt_ref, apow_ref, wglu_ref, bglu_ref, o_ref,
              st_ref, y_ref, tok_ref, *, nb, n_chunks):
    lc = SSM_CHUNK
    half = STATE_COLS // 2
    rows = nb * n_chunks
    u2 = u_ref[...].reshape(rows, lc * SSM_WIDTH)
    ug = ug_ref[...].reshape(rows, lc * SSM_WIDTH)
    pw = 2 * SSM_STATE
    for k in range(SSM_GROUPS // 2):
        res = jnp.dot(ug[:, 2 * k * pw:2 * (k + 1) * pw], bst_ref[k], preferred_element_type=F32)
        st_ref[:, k * pw:(k + 1) * pw] = res[:, :pw]
        st_ref[:, half + k * pw:half + (k + 1) * pw] = res[:, pw:]

    a_r = apow_ref[:, 0:half]
    a_i = apow_ref[:, half:]

    hw = lc * SSM_WIDTH // 2
    for half_idx in range(2):
        cols = slice(half_idx * hw, (half_idx + 1) * hw)
        kdim = (half_idx + 1) * hw
        y_ref[:, cols] = jnp.dot(u2[:, :kdim], wt_ref[0:kdim, cols], preferred_element_type=F32)

    state = [jnp.zeros((1, half), F32) for _ in range(2 * nb)]
    for c in range(n_chunks):
        for b in range(nb):
            x_r, x_i = state[2 * b], state[2 * b + 1]
            row = slice(b * n_chunks + c, b * n_chunks + c + 1)
            s_r = st_ref[row, 0:half]
            s_i = st_ref[row, half:]
            st_ref[row, 0:half] = x_r
            st_ref[row, half:] = x_i
            state[2 * b] = a_r * x_r - a_i * x_i + s_r
            state[2 * b + 1] = a_r * x_i + a_i * x_r + s_i

    x_in = st_ref[...].astype(BF16)
    for half_idx in range(2):
        cols = slice(half_idx * hw, (half_idx + 1) * hw)
        y_ref[:, cols] += jnp.dot(x_in, wc_ref[:, cols], preferred_element_type=F32)
    for j in range(lc):
        y = jax.nn.gelu(y_ref[:, j * SSM_WIDTH:(j + 1) * SSM_WIDTH])
        gate = jnp.dot(y.astype(BF16), wglu_ref[...], preferred_element_type=F32) + bglu_ref[...]
        res = y * jax.nn.sigmoid(gate)
        for h in range(SSM_WIDTH // LANES):
            tok_ref[h, pl.ds(j, rows, stride=lc), :] = res[:, h * LANES:(h + 1) * LANES]
    out = jnp.concatenate([tok_ref[h] for h in range(SSM_WIDTH // LANES)], axis=1)
    o_ref[...] = out.astype(BF16).reshape(nb, n_chunks * lc, SSM_WIDTH)


def _ssm(u2, ug, bst, carry, toeplitz, a_pow, w_glu, b_glu, nb):
    bsz, n_chunks, _ = u2.shape
    lc = SSM_CHUNK
    seq = n_chunks * lc
    return pl.pallas_call(
        functools.partial(_ssm_body, nb=nb, n_chunks=n_chunks),
        grid=(bsz // nb,),
        in_specs=[pl.BlockSpec((nb, n_chunks, lc * SSM_WIDTH), lambda b: (b, 0, 0)),
                  pl.BlockSpec((nb, n_chunks, lc * SSM_WIDTH), lambda b: (b, 0, 0)),
                  _const_spec(bst.shape), _const_spec(carry.shape), _const_spec(toeplitz.shape),
                  _const_spec(a_pow.shape),
                  _const_spec((SSM_WIDTH, SSM_WIDTH)), _const_spec((1, SSM_WIDTH))],
        out_specs=pl.BlockSpec((nb, seq, SSM_WIDTH), lambda b: (b, 0, 0)),
        out_shape=jax.ShapeDtypeStruct((bsz, seq, SSM_WIDTH), BF16),
        scratch_shapes=[pltpu.VMEM((nb * n_chunks, STATE_COLS), F32),
                        pltpu.VMEM((nb * n_chunks, lc * SSM_WIDTH), F32),
                        pltpu.VMEM((SSM_WIDTH // LANES, nb * seq, LANES), F32)],
        compiler_params=_params(1),
        name="ssm",
    )(u2, ug, bst, carry, toeplitz, a_pow, w_glu.astype(BF16), b_glu.reshape(1, SSM_WIDTH))


def _tail_body(x_ref, oatt_ref, sg_ref, sa_ref, ss_ref,
               mod_ref, wpa_ref, wps_ref, wo_ref, gf_ref, wup_ref, wc_ref, bc_ref,
               wdn_ref, gfin_ref, out_ref, carry_ref):
    tm = x_ref.shape[0]

    @pl.when(pl.program_id(1) == 0)
    def _():
        carry_ref[...] = jnp.zeros_like(carry_ref)

    y_att = jnp.dot(oatt_ref[...], wpa_ref[...], preferred_element_type=F32)
    y_ssm = jnp.dot(sg_ref[...], wps_ref[...], preferred_element_type=F32)
    merged = sa_ref[...].astype(F32) * y_att + ss_ref[...].astype(F32) * y_ssm
    mix = jnp.dot(merged.astype(BF16), wo_ref[...], preferred_element_type=F32)
    h1 = x_ref[...] + mod_ref[2:3, :] * mix

    u = _rms_modulate(h1, gf_ref[...], mod_ref[3:4, :], mod_ref[4:5, :]).astype(BF16)
    a = jnp.dot(u, wup_ref[:, 0:D_FF], preferred_element_type=F32)
    val = jnp.dot(u, wup_ref[:, D_FF:], preferred_element_type=F32)

    row = lax.broadcasted_iota(jnp.int32, a.shape, 0)
    prev1 = carry_ref[SUBLANES - 1:SUBLANES, :]
    prev2 = carry_ref[SUBLANES - 2:SUBLANES - 1, :]
    a1 = jnp.where(row == 0, prev1, pltpu.roll(a, 1, 0))
    a2 = jnp.where(row == 0, prev2, jnp.where(row == 1, prev1, pltpu.roll(a, 2, 0)))
    carry_ref[...] = a[tm - SUBLANES:, :]
    conv = bc_ref[...] + wc_ref[0:1, :] * a + wc_ref[1:2, :] * a1 + wc_ref[2:3, :] * a2
    act = (conv * jax.nn.sigmoid(conv) * val).astype(BF16)
    ffn = jnp.dot(act, wdn_ref[...], preferred_element_type=F32)
    h2 = h1 + mod_ref[5:6, :] * ffn

    ms = jnp.mean(h2 * h2, axis=-1, keepdims=True)
    out_ref[...] = (h2 * lax.rsqrt(ms + EPS)) * gfin_ref[...]


def _tail(x, o_att, sglu, sig_att, sig_ssm, mod3, w_proj_att, w_proj_ssm, w_out,
          g_ffn, w_up, w_conv, b_conv, w_down, g_final, tm):
    bsz, seq, d = x.shape

    def tok(width):
        return pl.BlockSpec((None, tm, width), lambda b, s: (b, s, 0))

    in_specs = [tok(d), tok(ATT_WIDTH), tok(SSM_WIDTH), tok(d), tok(d),
                pl.BlockSpec((None, 6, d), lambda b, s: (b, 0, 0)),
                _const_spec((ATT_WIDTH, d)), _const_spec((SSM_WIDTH, d)), _const_spec((d, d)),
                _const_spec((1, d)), _const_spec((d, 2 * D_FF)), _const_spec((CONV_W, D_FF)),
                _const_spec((1, D_FF)), _const_spec((D_FF, d)), _const_spec((1, d))]
    return pl.pallas_call(
        _tail_body,
        grid=(bsz, seq // tm),
        in_specs=in_specs,
        out_specs=tok(d),
        out_shape=jax.ShapeDtypeStruct((bsz, seq, d), F32),
        scratch_shapes=[pltpu.VMEM((SUBLANES, D_FF), F32)],
        compiler_params=_params(2, TAIL_VMEM_LIMIT_BYTES),
        name="tail",
    )(x, o_att, sglu, sig_att, sig_ssm, mod3,
      w_proj_att.astype(BF16), w_proj_ssm.astype(BF16), w_out.astype(BF16),
      g_ffn.reshape(1, d), w_up.astype(BF16), w_conv, b_conv.reshape(1, D_FF),
      w_down.astype(BF16), g_final.reshape(1, d))


def kernel(x, c, w_ada, b_ada, g_mix, w_in, b_gate, a_re, a_im, log_dt, b_re, b_im, c_re, c_im,
           d_skip, w_glu, b_glu, w_proj_att, w_proj_ssm, w_out, g_ffn, w_up, w_conv, b_conv,
           w_down, g_final):
    depth = w_ada.shape[0]
    assert depth == 1, "the final RMSNorm is fused into the single layer's tail kernel"
    bsz, seq, d = x.shape
    l = 0
    mod3 = _ada(c, w_ada[l], b_ada[l]).reshape(bsz, 6, d)
    q, k, v, us, ug, sig_att, sig_ssm = _inproj(x, mod3, g_mix[l], w_in[l], b_gate[l], tm=1024)

    o_att = _attention(q, k, v)

    ssm_mats = _ssm_matrices(a_re[l], a_im[l], log_dt[l], b_re[l], b_im[l], c_re[l], c_im[l],
                             d_skip[l])
    sglu = _ssm(us, ug, *ssm_mats, w_glu[l], b_glu[l], nb=2)

    return _tail(x, o_att, sglu, sig_att, sig_ssm, mod3, w_proj_att[l], w_proj_ssm[l],
                 w_out[l], g_ffn[l], w_up[l], w_conv[l], b_conv[l], w_down[l], g_final, tm=1024)
```

```python
import functools
import math

import jax
import jax.numpy as jnp
import numpy as np
from jax import lax
from jax.experimental import pallas as pl
from jax.experimental.pallas import tpu as pltpu

D_MODEL = 1024
N_HEADS = 8
HEAD_DIM = 64
ATT_WIDTH = N_HEADS * HEAD_DIM
PATTERNS = ((128, 1), (512, 4), (2048, 16))
SSM_GROUPS = 16
SSM_GROUP_CH = 16
SSM_WIDTH = SSM_GROUPS * SSM_GROUP_CH
SSM_STATE = 64
D_FF = 2048
CONV_W = 3
EPS = 1e-6
NEG_INF = -1e30
LOG2E = math.log2(math.e)

LANES = 128
SUBLANES = 8
VMEM_BYTES = 64 * 1024 * 1024
VMEM_LIMIT_BYTES = 56 * 1024 * 1024
TAIL_VMEM_LIMIT_BYTES = VMEM_BYTES - 4 * 1024 * 1024

ATT_BLOCK = 128
SSM_CHUNK = 8
STATE_COLS = 2 * SSM_GROUPS * SSM_STATE

BF16 = jnp.bfloat16
F32 = jnp.float32


def _const_spec(shape):
    zeros = (0,) * len(shape)
    return pl.BlockSpec(shape, lambda *_: zeros, pipeline_mode=pl.Buffered(1))


def _params(n_axes, vmem_limit_bytes=VMEM_LIMIT_BYTES):
    return pltpu.CompilerParams(
        dimension_semantics=("arbitrary",) * n_axes,
        vmem_limit_bytes=vmem_limit_bytes)


def _ada_body(c_ref, w_ref, b_ref, o_ref):
    c = c_ref[...]
    act = (c * jax.nn.sigmoid(c)).astype(BF16)
    o_ref[...] = jnp.dot(act, w_ref[...], preferred_element_type=F32) + b_ref[...]


def _ada(c, w_ada, b_ada):
    bsz = c.shape[0]
    n_out = w_ada.shape[1]
    tn = 1536
    return pl.pallas_call(
        _ada_body,
        grid=(n_out // tn,),
        in_specs=[_const_spec((bsz, D_MODEL)),
                  pl.BlockSpec((D_MODEL, tn), lambda j: (0, j)),
                  pl.BlockSpec((1, tn), lambda j: (0, j))],
        out_specs=pl.BlockSpec((bsz, tn), lambda j: (0, j)),
        out_shape=jax.ShapeDtypeStruct((bsz, n_out), F32),
        compiler_params=_params(1),
        name="ada",
    )(c, w_ada.astype(BF16), b_ada.reshape(1, n_out))


def _rms_modulate(x, gain, shift, scale):
    ms = jnp.mean(x * x, axis=-1, keepdims=True)
    return (x * lax.rsqrt(ms + EPS)) * (gain * (1.0 + scale)) + shift


def _inproj_body(x_ref, mod_ref, g_ref, w_ref, bg_ref,
                 q_ref, k_ref, v_ref, us_ref, ug_ref, sa_ref, ss_ref, us_scr):
    u = _rms_modulate(x_ref[...], g_ref[...], mod_ref[0:1, :], mod_ref[1:2, :]).astype(BF16)

    def proj(lo, hi):
        return jnp.dot(u, w_ref[:, lo:hi], preferred_element_type=F32)

    a = ATT_WIDTH
    q_ref[...] = (proj(0, a) * (HEAD_DIM ** -0.5)).astype(BF16)
    k_ref[...] = proj(a, 2 * a).astype(BF16)
    v_ref[...] = proj(2 * a, 3 * a).astype(BF16)
    o = 3 * a
    us = proj(o, o + SSM_WIDTH)
    n_tiles = SSM_WIDTH // LANES
    for h in range(n_tiles):
        us_scr[h] = us[:, h * LANES:(h + 1) * LANES]
    n_rows = us_scr.shape[1] // SSM_CHUNK
    pieces = [[us_scr[h, pl.ds(s, n_rows, stride=SSM_CHUNK), :] for h in range(n_tiles)]
              for s in range(SSM_CHUNK)]
    us_ref[...] = jnp.concatenate(
        [pieces[s][h] for s in range(SSM_CHUNK) for h in range(n_tiles)], axis=1).astype(BF16)
    per_tile = LANES // SSM_GROUP_CH
    ug_ref[...] = jnp.concatenate(
        [pieces[s][g // per_tile][:, (g % per_tile) * SSM_GROUP_CH:(g % per_tile + 1) * SSM_GROUP_CH]
         for g in range(SSM_GROUPS) for s in range(SSM_CHUNK)], axis=1).astype(BF16)
    o += SSM_WIDTH
    sa_ref[...] = jax.nn.sigmoid(proj(o, o + D_MODEL) + bg_ref[:, 0:D_MODEL]).astype(BF16)
    o += D_MODEL
    ss_ref[...] = jax.nn.sigmoid(proj(o, o + D_MODEL) + bg_ref[:, D_MODEL:]).astype(BF16)


def _inproj(x, mod3, g_mix, w_in, b_gate, tm):
    bsz, seq, d = x.shape
    in_width = w_in.shape[1]

    def tok(width):
        return pl.BlockSpec((None, tm, width), lambda b, s: (b, s, 0))

    def out(width):
        return jax.ShapeDtypeStruct((bsz, seq, width), BF16)

    return pl.pallas_call(
        _inproj_body,
        grid=(bsz, seq // tm),
        in_specs=[tok(d),
                  pl.BlockSpec((None, 6, d), lambda b, s: (b, 0, 0)),
                  _const_spec((1, d)),
                  _const_spec((d, in_width)),
                  _const_spec((1, 2 * d))],
        out_specs=[tok(ATT_WIDTH), tok(ATT_WIDTH), tok(ATT_WIDTH),
                   pl.BlockSpec((None, tm // SSM_CHUNK, SSM_CHUNK * SSM_WIDTH), lambda b, s: (b, s, 0)),
                   pl.BlockSpec((None, tm // SSM_CHUNK, SSM_CHUNK * SSM_WIDTH), lambda b, s: (b, s, 0)),
                   tok(d), tok(d)],
        out_shape=[out(ATT_WIDTH), out(ATT_WIDTH), out(ATT_WIDTH),
                   jax.ShapeDtypeStruct((bsz, seq // SSM_CHUNK, SSM_CHUNK * SSM_WIDTH), BF16),
                   jax.ShapeDtypeStruct((bsz, seq // SSM_CHUNK, SSM_CHUNK * SSM_WIDTH), BF16),
                   out(d), out(d)],
        scratch_shapes=[pltpu.VMEM((SSM_WIDTH // LANES, tm, LANES), F32)],
        compiler_params=_params(2),
        name="inproj",
    )(x, mod3, g_mix.reshape(1, d), w_in.astype(BF16), b_gate.reshape(1, 2 * d))


def _alibi_distance_table(dilation, kw):
    w = ATT_BLOCK
    j = np.arange(kw)[:, None]
    a = np.arange(w)[None, :]
    tabs = []
    for sel in range(2):
        dist = (sel * w + a - j).astype(np.float32)
        valid = (dist >= 0) & (dist <= w)
        tabs.append(np.where(valid, -dilation * dist, NEG_INF))
    return np.stack(tabs).astype(np.float32)


def _alibi_slope_table():
    w = ATT_BLOCK
    tabs = np.zeros((N_HEADS // 2, 2 * w, w), np.float32)
    for h in range(N_HEADS):
        slope = 2.0 ** (-8.0 * (h + 1) / N_HEADS)
        tabs[h // 2, (h % 2) * w:(h % 2 + 1) * w, :] = slope * np.eye(w, dtype=np.float32)
    return tabs


def _pair_block(qp, kp, vp, slopes, dist_t, low_half):
    w = ATT_BLOCK
    kw = kp.shape[0]
    zero = jnp.zeros((), BF16)
    qs = jnp.concatenate([jnp.where(low_half, qp, zero), jnp.where(low_half, zero, qp)], axis=0)
    lhs = jnp.concatenate([qs, slopes], axis=1)
    rhs = jnp.concatenate([kp, dist_t], axis=1)
    s2 = lax.dot_general(lhs, rhs, (((1,), (1,)), ((), ())), preferred_element_type=F32)
    v_ones = jnp.concatenate([vp, jnp.ones((kw, LANES), BF16)], axis=1)
    res, maxes = [], []
    for hh in range(2):
        s = s2[hh * w:(hh + 1) * w]
        m = jnp.max(s, axis=-1, keepdims=True)
        p = jnp.exp2(((s - m) * LOG2E).astype(BF16))
        res.append(jnp.dot(p, v_ones, preferred_element_type=F32))
        maxes.append(m)
    num = jnp.where(low_half, res[0][:, :LANES], res[1][:, :LANES])
    den = jnp.where(low_half, res[0][:, LANES:], res[1][:, LANES:])
    return num, den, jnp.where(low_half, maxes[0], maxes[1])


def _attn_body(q_ref, k_ref, v_ref, slope_ref, b1_ref, b4_ref, b16_ref, o_ref,
               f32_ref, sub4f_ref, sub4_ref, sub16_ref, num_ref, den_ref, max_ref, *, seq):
    w = ATT_BLOCK
    pair = pl.program_id(1)
    low_half = lax.broadcasted_iota(jnp.int32, (w, LANES), 1) < HEAD_DIM
    slopes = slope_ref[pair]
    srcs = (q_ref, k_ref, v_ref)
    n4, n16 = seq // 4, seq // 16
    for t in range(3):
        f32_ref[t] = srcs[t][...].astype(F32)
    for t in range(3):
        for r4 in range(4):
            sub4f_ref[t, r4 * n4:(r4 + 1) * n4, :] = f32_ref[t, pl.ds(r4, n4, stride=4), :]
    for t in range(3):
        sub4_ref[t] = sub4f_ref[t].astype(BF16)
        for r4 in range(4):
            for q4 in range(4):
                r16 = 4 * q4 + r4
                sub16_ref[t, r16 * n16:(r16 + 1) * n16, :] = (
                    sub4f_ref[t, pl.ds(r4 * n4 + q4, n16, stride=4), :].astype(BF16))

    def block(load, bias_ref, sub_len, r, i):
        base = r * sub_len
        kw = w if i == 0 else 2 * w
        krow = base + max(i - 1, 0) * w
        sel = min(i, 1)
        return _pair_block(load(0, base + i * w, w), load(1, krow, kw), load(2, krow, kw),
                           slopes, bias_ref[sel, 0:kw, :], low_half)

    dilated = ((0, 4, b4_ref, lambda t, row, size: sub4_ref[t, row:row + size, :]),
               (1, 16, b16_ref, lambda t, row, size: sub16_ref[t, row:row + size, :]))
    for pidx, dilation, bias_ref, load in dilated:
        sub_len = seq // dilation
        for r in range(dilation):
            for i in range(sub_len // w):
                num, den, m = block(load, bias_ref, sub_len, r, i)
                dst = pl.ds(dilation * w * i + r, w, stride=dilation)
                num_ref[pidx, dst, :] = num
                den_ref[pidx, dst, :] = den
                max_ref[pidx, dst, :] = m

    for i in range(seq // w):
        num1, den1, m1 = block(lambda t, row, size: srcs[t][row:row + size, :], b1_ref, seq, 0, i)
        rows = slice(i * w, (i + 1) * w)
        m2, m3 = max_ref[0, rows, :], max_ref[1, rows, :]
        mx = jnp.maximum(jnp.maximum(m1, m2), m3)
        e1, e2, e3 = jnp.exp(m1 - mx), jnp.exp(m2 - mx), jnp.exp(m3 - mx)
        num = e1 * num1 + e2 * num_ref[0, rows, :] + e3 * num_ref[1, rows, :]
        den = e1 * den1 + e2 * den_ref[0, rows, :] + e3 * den_ref[1, rows, :]
        o_ref[rows, :] = (num * (1.0 / den)).astype(BF16)


def _attention(q, k, v):
    bsz, seq, _ = q.shape
    w = ATT_BLOCK
    slopes = jnp.asarray(_alibi_slope_table(), BF16)
    biases = [jnp.asarray(_alibi_distance_table(d, min(2 * w, seq // d)), BF16) for _, d in PATTERNS]
    blk = pl.BlockSpec((None, seq, LANES), lambda b, p: (b, 0, p))
    return pl.pallas_call(
        functools.partial(_attn_body, seq=seq),
        grid=(bsz, N_HEADS // 2),
        in_specs=[blk, blk, blk, _const_spec(slopes.shape)] + [_const_spec(t.shape) for t in biases],
        out_specs=blk,
        out_shape=jax.ShapeDtypeStruct((bsz, seq, ATT_WIDTH), BF16),
        scratch_shapes=[pltpu.VMEM((3, seq, LANES), F32),
                        pltpu.VMEM((3, seq, LANES), F32),
                        pltpu.VMEM((3, seq, LANES), BF16),
                        pltpu.VMEM((3, seq, LANES), BF16),
                        pltpu.VMEM((2, seq, LANES), F32),
                        pltpu.VMEM((2, seq, LANES), F32),
                        pltpu.VMEM((2, seq, LANES), F32)],
        compiler_params=_params(2),
        name="attn",
    )(q, k, v, slopes, *biases)


def _placement(n_outer, n_inner):
    t = np.zeros((n_outer, n_inner, n_outer, SSM_GROUPS, n_inner), np.float32)
    for o in range(n_outer):
        for i in range(n_inner):
            t[o, i, o, :, i] = 1.0
    return t.reshape(n_outer * n_inner, n_outer * SSM_GROUPS * n_inner)


def _block_diagonal(val, n_outer, n_inner, row_inner):
    full = jnp.dot(val.astype(BF16), jnp.asarray(_placement(n_outer, n_inner)).astype(BF16),
                   preferred_element_type=F32)
    row_g = (lax.broadcasted_iota(jnp.int32, full.shape, 0) // row_inner) % SSM_GROUPS
    col_g = (lax.broadcasted_iota(jnp.int32, full.shape, 1) // n_inner) % SSM_GROUPS
    return jnp.where(row_g == col_g, full, 0.0).astype(BF16)


def _ssm_matrices(a_re, a_im, log_dt, b_re, b_im, c_re, c_im, d_skip):
    lc, g_n, n_st, ch = SSM_CHUNK, SSM_GROUPS, SSM_STATE, SSM_GROUP_CH
    hi = lax.Precision.HIGHEST
    lr, li = a_re.astype(F32), a_im.astype(F32)
    dt = jnp.exp(log_dt.astype(F32))[:, None]
    mag = jnp.exp(lr * dt)
    ang = li * dt
    ab_re, ab_im = mag * jnp.cos(ang), mag * jnp.sin(ang)
    nr, ni = ab_re - 1.0, ab_im
    den = lr * lr + li * li
    f_re = (nr * lr + ni * li) / den
    f_im = (ni * lr - nr * li) / den
    bb_re = f_re[..., None] * b_re - f_im[..., None] * b_im
    bb_im = f_re[..., None] * b_im + f_im[..., None] * b_re

    kk = jnp.arange(lc + 1, dtype=F32)[:, None, None]
    pw_mag = jnp.exp(kk * (lr * dt)[None])
    pw_re = pw_mag * jnp.cos(kk * ang[None])
    pw_im = pw_mag * jnp.sin(kk * ang[None])

    cp_re = c_re[None] * pw_re[:, :, None, :] - c_im[None] * pw_im[:, :, None, :]
    cp_im = c_re[None] * pw_im[:, :, None, :] + c_im[None] * pw_re[:, :, None, :]

    taps = (jnp.einsum('kgcn,gni->kgci', cp_re[:lc], bb_re, precision=hi)
            - jnp.einsum('kgcn,gni->kgci', cp_im[:lc], bb_im, precision=hi))
    skip = d_skip.reshape(g_n, ch)
    taps = taps.at[0].add(skip[:, :, None] * jnp.eye(ch, dtype=F32)[None])
    lag = np.arange(lc)[None, :] - np.arange(lc)[:, None]
    tz = taps[np.maximum(lag, 0)] * jnp.asarray(lag >= 0, F32)[:, :, None, None, None]
    tz = tz.transpose(0, 2, 4, 1, 3).reshape(lc * SSM_WIDTH, lc * ch)
    toeplitz = _block_diagonal(tz, lc, ch, ch)

    cv = jnp.stack([cp_re[1:], -cp_im[1:]], axis=0)
    cv = cv.transpose(0, 2, 4, 1, 3).reshape(STATE_COLS, lc * ch)
    carry = _block_diagonal(cv, lc, ch, n_st)

    rp_re, rp_im = pw_re[:lc][::-1], pw_im[:lc][::-1]
    bp_re = rp_re[..., None] * bb_re[None] - rp_im[..., None] * bb_im[None]
    bp_im = rp_re[..., None] * bb_im[None] + rp_im[..., None] * bb_re[None]
    bv = jnp.stack([bp_re, bp_im], axis=0).reshape(2, lc, g_n // 2, 2, n_st, ch)
    bst = jnp.einsum('pskinc,ij->kiscpjn', bv, jnp.eye(2, dtype=F32))
    bst = bst.reshape(g_n // 2, 2 * lc * ch, 4 * n_st).astype(BF16)

    half = g_n * n_st
    a_pow = jnp.concatenate([pw_re[lc].reshape(1, half), pw_im[lc].reshape(1, half)], axis=1)
    return bst, carry, toeplitz, a_pow


def _ssm_body(u_ref, ug_ref, bst_ref, wc_ref, wt_ref, apow_ref, wglu_ref, bglu_ref, o_ref,
              st_ref, y_ref, tok_ref, *, nb, n_chunks):
    lc = SSM_CHUNK
    half = STATE_COLS // 2
    rows = nb * n_chunks
    u2 = u_ref[...].reshape(rows, lc * SSM_WIDTH)
    ug = ug_ref[...].reshape(rows, lc * SSM_WIDTH)
    pw = 2 * SSM_STATE
    for k in range(SSM_GROUPS // 2):
        res = jnp.dot(ug[:, 2 * k * pw:2 * (k + 1) * pw], bst_ref[k], preferred_element_type=F32)
        st_ref[:, k * pw:(k + 1) * pw] = res[:, :pw]
        st_ref[:, half + k * pw:half + (k + 1) * pw] = res[:, pw:]

    a_r = apow_ref[:, 0:half]
    a_i = apow_ref[:, half:]

    hw = lc * SSM_WIDTH // 2
    for half_idx in range(2):
        cols = slice(half_idx * hw, (half_idx + 1) * hw)
        kdim = (half_idx + 1) * hw
        y_ref[:, cols] = jnp.dot(u2[:, :kdim], wt_ref[0:kdim, cols], preferred_element_type=F32)

    state = [jnp.zeros((1, half), F32) for _ in range(2 * nb)]
    for c in range(n_chunks):
        for b in range(nb):
            x_r, x_i = state[2 * b], state[2 * b + 1]
            row = slice(b * n_chunks + c, b * n_chunks + c + 1)
            s_r = st_ref[row, 0:half]
            s_i = st_ref[row, half:]
            st_ref[row, 0:half] = x_r
            st_ref[row, half:] = x_i
            state[2 * b] = a_r * x_r - a_i * x_i + s_r
            state[2 * b + 1] = a_r * x_i + a_i * x_r + s_i

    x_in = st_ref[...].astype(BF16)
    for half_idx in range(2):
        cols = slice(half_idx * hw, (half_idx + 1) * hw)
        y_ref[:, cols] += jnp.dot(x_in, wc_ref[:, cols], preferred_element_type=F32)
    for j in range(lc):
        y = jax.nn.gelu(y_ref[:, j * SSM_WIDTH:(j + 1) * SSM_WIDTH])
        gate = jnp.dot(y.astype(BF16), wglu_ref[...], preferred_element_type=F32) + bglu_ref[...]
        res = y * jax.nn.sigmoid(gate)
        for h in range(SSM_WIDTH // LANES):
            tok_ref[h, pl.ds(j, rows, stride=lc), :] = res[:, h * LANES:(h + 1) * LANES]
    out = jnp.concatenate([tok_ref[h] for h in range(SSM_WIDTH // LANES)], axis=1)
    o_ref[...] = out.astype(BF16).reshape(nb, n_chunks * lc, SSM_WIDTH)


def _ssm(u2, ug, bst, carry, toeplitz, a_pow, w_glu, b_glu, nb):
    bsz, n_chunks, _ = u2.shape
    lc = SSM_CHUNK
    seq = n_chunks * lc
    return pl.pallas_call(
        functools.partial(_ssm_body, nb=nb, n_chunks=n_chunks),
        grid=(bsz // nb,),
        in_specs=[pl.BlockSpec((nb, n_chunks, lc * SSM_WIDTH), lambda b: (b, 0, 0)),
                  pl.BlockSpec((nb, n_chunks, lc * SSM_WIDTH), lambda b: (b, 0, 0)),
                  _const_spec(bst.shape), _const_spec(carry.shape), _const_spec(toeplitz.shape),
                  _const_spec(a_pow.shape),
                  _const_spec((SSM_WIDTH, SSM_WIDTH)), _const_spec((1, SSM_WIDTH))],
        out_specs=pl.BlockSpec((nb, seq, SSM_WIDTH), lambda b: (b, 0, 0)),
        out_shape=jax.ShapeDtypeStruct((bsz, seq, SSM_WIDTH), BF16),
        scratch_shapes=[pltpu.VMEM((nb * n_chunks, STATE_COLS), F32),
                        pltpu.VMEM((nb * n_chunks, lc * SSM_WIDTH), F32),
                        pltpu.VMEM((SSM_WIDTH // LANES, nb * seq, LANES), F32)],
        compiler_params=_params(1),
        name="ssm",
    )(u2, ug, bst, carry, toeplitz, a_pow, w_glu.astype(BF16), b_glu.reshape(1, SSM_WIDTH))


def _tail_body(x_ref, oatt_ref, sg_ref, sa_ref, ss_ref,
               mod_ref, wpa_ref, wps_ref, wo_ref, gf_ref, wup_ref, wc_ref, bc_ref,
               wdn_ref, gfin_ref, out_ref, carry_ref):
    tm = x_ref.shape[0]

    @pl.when(pl.program_id(1) == 0)
    def _():
        carry_ref[...] = jnp.zeros_like(carry_ref)

    y_att = jnp.dot(oatt_ref[...], wpa_ref[...], preferred_element_type=F32)
    y_ssm = jnp.dot(sg_ref[...], wps_ref[...], preferred_element_type=F32)
    merged = sa_ref[...].astype(F32) * y_att + ss_ref[...].astype(F32) * y_ssm
    mix = jnp.dot(merged.astype(BF16), wo_ref[...], preferred_element_type=F32)
    h1 = x_ref[...] + mod_ref[2:3, :] * mix

    u = _rms_modulate(h1, gf_ref[...], mod_ref[3:4, :], mod_ref[4:5, :]).astype(BF16)
    a = jnp.dot(u, wup_ref[:, 0:D_FF], preferred_element_type=F32)
    val = jnp.dot(u, wup_ref[:, D_FF:], preferred_element_type=F32)

    row = lax.broadcasted_iota(jnp.int32, a.shape, 0)
    prev1 = carry_ref[SUBLANES - 1:SUBLANES, :]
    prev2 = carry_ref[SUBLANES - 2:SUBLANES - 1, :]
    a1 = jnp.where(row == 0, prev1, pltpu.roll(a, 1, 0))
    a2 = jnp.where(row == 0, prev2, jnp.where(row == 1, prev1, pltpu.roll(a, 2, 0)))
    carry_ref[...] = a[tm - SUBLANES:, :]
    conv = bc_ref[...] + wc_ref[0:1, :] * a + wc_ref[1:2, :] * a1 + wc_ref[2:3, :] * a2
    act = (conv * jax.nn.sigmoid(conv) * val).astype(BF16)
    ffn = jnp.dot(act, wdn_ref[...], preferred_element_type=F32)
    h2 = h1 + mod_ref[5:6, :] * ffn

    ms = jnp.mean(h2 * h2, axis=-1, keepdims=True)
    out_ref[...] = (h2 * lax.rsqrt(ms + EPS)) * gfin_ref[...]


def _tail(x, o_att, sglu, sig_att, sig_ssm, mod3, w_proj_att, w_proj_ssm, w_out,
          g_ffn, w_up, w_conv, b_conv, w_down, g_final, tm):
    bsz, seq, d = x.shape

    def tok(width):
        return pl.BlockSpec((None, tm, width), lambda b, s: (b, s, 0))

    in_specs = [tok(d), tok(ATT_WIDTH), tok(SSM_WIDTH), tok(d), tok(d),
                pl.BlockSpec((None, 6, d), lambda b, s: (b, 0, 0)),
                _const_spec((ATT_WIDTH, d)), _const_spec((SSM_WIDTH, d)), _const_spec((d, d)),
                _const_spec((1, d)), _const_spec((d, 2 * D_FF)), _const_spec((CONV_W, D_FF)),
                _const_spec((1, D_FF)), _const_spec((D_FF, d)), _const_spec((1, d))]
    return pl.pallas_call(
        _tail_body,
        grid=(bsz, seq // tm),
        in_specs=in_specs,
        out_specs=tok(d),
        out_shape=jax.ShapeDtypeStruct((bsz, seq, d), F32),
        scratch_shapes=[pltpu.VMEM((SUBLANES, D_FF), F32)],
        compiler_params=_params(2, TAIL_VMEM_LIMIT_BYTES),
        name="tail",
    )(x, o_att, sglu, sig_att, sig_ssm, mod3,
      w_proj_att.astype(BF16), w_proj_ssm.astype(BF16), w_out.astype(BF16),
      g_ffn.reshape(1, d), w_up.astype(BF16), w_conv, b_conv.reshape(1, D_FF),
      w_down.astype(BF16), g_final.reshape(1, d))


def kernel(x, c, w_ada, b_ada, g_mix, w_in, b_gate, a_re, a_im, log_dt, b_re, b_im, c_re, c_im,
           d_skip, w_glu, b_glu, w_proj_att, w_proj_ssm, w_out, g_ffn, w_up, w_conv, b_conv,
           w_down, g_final):
    depth = w_ada.shape[0]
    assert depth == 1, "the final RMSNorm is fused into the single layer's tail kernel"
    bsz, seq, d = x.shape
    l = 0
    mod3 = _ada(c, w_ada[l], b_ada[l]).reshape(bsz, 6, d)
    q, k, v, us, ug, sig_att, sig_ssm = _inproj(x, mod3, g_mix[l], w_in[l], b_gate[l], tm=1024)

    o_att = _attention(q, k, v)

    ssm_mats = _ssm_matrices(a_re[l], a_im[l], log_dt[l], b_re[l], b_im[l], c_re[l], c_im[l],
                             d_skip[l])
    sglu = _ssm(us, ug, *ssm_mats, w_glu[l], b_glu[l], nb=2)

    return _tail(x, o_att, sglu, sig_att, sig_ssm, mod3, w_proj_att[l], w_proj_ssm[l],
                 w_out[l], g_ffn[l], w_up[l], w_conv[l], b_conv[l], w_down[l], g_final, tm=1024)
```

```python
import functools
import math

import jax
import jax.numpy as jnp
import numpy as np
from jax import lax
from jax.experimental import pallas as pl
from jax.experimental.pallas import tpu as pltpu

D_MODEL = 1024
N_HEADS = 8
HEAD_DIM = 64
ATT_WIDTH = N_HEADS * HEAD_DIM
PATTERNS = ((128, 1), (512, 4), (2048, 16))
SSM_GROUPS = 16
SSM_GROUP_CH = 16
SSM_WIDTH = SSM_GROUPS * SSM_GROUP_CH
SSM_STATE = 64
D_FF = 2048
CONV_W = 3
EPS = 1e-6
NEG_INF = -1e30
LOG2E = math.log2(math.e)

LANES = 128
SUBLANES = 8
VMEM_BYTES = 64 * 1024 * 1024
VMEM_LIMIT_BYTES = 56 * 1024 * 1024
TAIL_VMEM_LIMIT_BYTES = VMEM_BYTES - 4 * 1024 * 1024

ATT_BLOCK = 128
SSM_CHUNK = 8
INPROJ_ROW_SLICES = 8
STATE_COLS = 2 * SSM_GROUPS * SSM_STATE

BF16 = jnp.bfloat16
F32 = jnp.float32


def _const_spec(shape):
    zeros = (0,) * len(shape)
    return pl.BlockSpec(shape, lambda *_: zeros, pipeline_mode=pl.Buffered(1))


def _params(n_axes, vmem_limit_bytes=VMEM_LIMIT_BYTES):
    return pltpu.CompilerParams(
        dimension_semantics=("arbitrary",) * n_axes,
        vmem_limit_bytes=vmem_limit_bytes)


def _ada_body(c_ref, w_ref, b_ref, o_ref):
    c = c_ref[...]
    act = (c * jax.nn.sigmoid(c)).astype(BF16)
    o_ref[...] = jnp.dot(act, w_ref[...], preferred_element_type=F32) + b_ref[...]


def _ada(c, w_ada, b_ada):
    bsz = c.shape[0]
    n_out = w_ada.shape[1]
    tn = 1536
    return pl.pallas_call(
        _ada_body,
        grid=(n_out // tn,),
        in_specs=[_const_spec((bsz, D_MODEL)),
                  pl.BlockSpec((D_MODEL, tn), lambda j: (0, j)),
                  pl.BlockSpec((1, tn), lambda j: (0, j))],
        out_specs=pl.BlockSpec((bsz, tn), lambda j: (0, j)),
        out_shape=jax.ShapeDtypeStruct((bsz, n_out), F32),
        compiler_params=_params(1),
        name="ada",
    )(c, w_ada.astype(BF16), b_ada.reshape(1, n_out))


def _rms_modulate(x, gain, shift, scale):
    ms = jnp.mean(x * x, axis=-1, keepdims=True)
    return (x * lax.rsqrt(ms + EPS)) * (gain * (1.0 + scale)) + shift


def _inproj_body(x_ref, mod_ref, g_ref, w_ref, bg_ref,
                 q_ref, k_ref, v_ref, us_ref, ug_ref, sa_ref, ss_ref, us_scr):
    tm = x_ref.shape[0]
    hm = tm // INPROJ_ROW_SLICES
    a = ATT_WIDTH
    n_tiles = SSM_WIDTH // LANES
    for part in range(INPROJ_ROW_SLICES):
        rows = slice(part * hm, (part + 1) * hm)
        u = _rms_modulate(x_ref[rows, :], g_ref[...], mod_ref[0:1, :], mod_ref[1:2, :]).astype(BF16)

        def proj(lo, hi):
            return jnp.dot(u, w_ref[:, lo:hi], preferred_element_type=F32)

        q_ref[rows, :] = (proj(0, a) * (HEAD_DIM ** -0.5)).astype(BF16)
        k_ref[rows, :] = proj(a, 2 * a).astype(BF16)
        v_ref[rows, :] = proj(2 * a, 3 * a).astype(BF16)
        o = 3 * a
        us = proj(o, o + SSM_WIDTH)
        for h in range(n_tiles):
            us_scr[h, rows, :] = us[:, h * LANES:(h + 1) * LANES]
        o += SSM_WIDTH
        sa_ref[rows, :] = jax.nn.sigmoid(proj(o, o + D_MODEL) + bg_ref[:, 0:D_MODEL]).astype(BF16)
        o += D_MODEL
        ss_ref[rows, :] = jax.nn.sigmoid(proj(o, o + D_MODEL) + bg_ref[:, D_MODEL:]).astype(BF16)
    n_rows = tm // SSM_CHUNK
    pieces = [[us_scr[h, pl.ds(s, n_rows, stride=SSM_CHUNK), :] for h in range(n_tiles)]
              for s in range(SSM_CHUNK)]
    us_ref[...] = jnp.concatenate(
        [pieces[s][h] for s in range(SSM_CHUNK) for h in range(n_tiles)], axis=1).astype(BF16)
    per_tile = LANES // SSM_GROUP_CH
    ug_ref[...] = jnp.concatenate(
        [pieces[s][g // per_tile][:, (g % per_tile) * SSM_GROUP_CH:(g % per_tile + 1) * SSM_GROUP_CH]
         for g in range(SSM_GROUPS) for s in range(SSM_CHUNK)], axis=1).astype(BF16)


def _inproj(x, mod3, g_mix, w_in, b_gate, tm):
    bsz, seq, d = x.shape
    in_width = w_in.shape[1]

    def tok(width):
        return pl.BlockSpec((None, tm, width), lambda b, s: (b, s, 0))

    def out(width):
        return jax.ShapeDtypeStruct((bsz, seq, width), BF16)

    return pl.pallas_call(
        _inproj_body,
        grid=(bsz, seq // tm),
        in_specs=[tok(d),
                  pl.BlockSpec((None, 6, d), lambda b, s: (b, 0, 0)),
                  _const_spec((1, d)),
                  _const_spec((d, in_width)),
                  _const_spec((1, 2 * d))],
        out_specs=[tok(ATT_WIDTH), tok(ATT_WIDTH), tok(ATT_WIDTH),
                   pl.BlockSpec((None, tm // SSM_CHUNK, SSM_CHUNK * SSM_WIDTH), lambda b, s: (b, s, 0)),
                   pl.BlockSpec((None, tm // SSM_CHUNK, SSM_CHUNK * SSM_WIDTH), lambda b, s: (b, s, 0)),
                   tok(d), tok(d)],
        out_shape=[out(ATT_WIDTH), out(ATT_WIDTH), out(ATT_WIDTH),
                   jax.ShapeDtypeStruct((bsz, seq // SSM_CHUNK, SSM_CHUNK * SSM_WIDTH), BF16),
                   jax.ShapeDtypeStruct((bsz, seq // SSM_CHUNK, SSM_CHUNK * SSM_WIDTH), BF16),
                   out(d), out(d)],
        scratch_shapes=[pltpu.VMEM((SSM_WIDTH // LANES, tm, LANES), F32)],
        compiler_params=_params(2),
        name="inproj",
    )(x, mod3, g_mix.reshape(1, d), w_in.astype(BF16), b_gate.reshape(1, 2 * d))


def _alibi_distance_table(dilation, kw):
    w = ATT_BLOCK
    j = np.arange(kw)[:, None]
    a = np.arange(w)[None, :]
    tabs = []
    for sel in range(2):
        dist = (sel * w + a - j).astype(np.float32)
        valid = (dist >= 0) & (dist <= w)
        tabs.append(np.where(valid, -dilation * dist, NEG_INF))
    return np.stack(tabs).astype(np.float32)


def _alibi_slope_table():
    w = ATT_BLOCK
    tabs = np.zeros((N_HEADS // 2, 2 * w, w), np.float32)
    for h in range(N_HEADS):
        slope = 2.0 ** (-8.0 * (h + 1) / N_HEADS)
        tabs[h // 2, (h % 2) * w:(h % 2 + 1) * w, :] = slope * np.eye(w, dtype=np.float32)
    return tabs


def _pair_block(qp, kp, vp, slopes, dist_t, low_half):
    w = ATT_BLOCK
    kw = kp.shape[0]
    zero = jnp.zeros((), BF16)
    qs = jnp.concatenate([jnp.where(low_half, qp, zero), jnp.where(low_half, zero, qp)], axis=0)
    lhs = jnp.concatenate([qs, slopes], axis=1)
    rhs = jnp.concatenate([kp, dist_t], axis=1)
    s2 = lax.dot_general(lhs, rhs, (((1,), (1,)), ((), ())), preferred_element_type=F32)
    v_ones = jnp.concatenate([vp, jnp.ones((kw, LANES), BF16)], axis=1)
    res, maxes = [], []
    for hh in range(2):
        s = s2[hh * w:(hh + 1) * w]
        m = jnp.max(s, axis=-1, keepdims=True)
        p = jnp.exp2(((s - m) * LOG2E).astype(BF16))
        res.append(jnp.dot(p, v_ones, preferred_element_type=F32))
        maxes.append(m)
    num = jnp.where(low_half, res[0][:, :LANES], res[1][:, :LANES])
    den = jnp.where(low_half, res[0][:, LANES:], res[1][:, LANES:])
    return num, den, jnp.where(low_half, maxes[0], maxes[1])


def _attn_body(q_ref, k_ref, v_ref, slope_ref, b1_ref, b4_ref, b16_ref, o_ref,
               f32_ref, sub4f_ref, sub4_ref, sub16_ref, num_ref, den_ref, max_ref, *, seq):
    w = ATT_BLOCK
    pair = pl.program_id(1)
    low_half = lax.broadcasted_iota(jnp.int32, (w, LANES), 1) < HEAD_DIM
    slopes = slope_ref[pair]
    srcs = (q_ref, k_ref, v_ref)
    n4, n16 = seq // 4, seq // 16
    for t in range(3):
        f32_ref[t] = srcs[t][...].astype(F32)
    for t in range(3):
        for r4 in range(4):
            sub4f_ref[t, r4 * n4:(r4 + 1) * n4, :] = f32_ref[t, pl.ds(r4, n4, stride=4), :]
    for t in range(3):
        sub4_ref[t] = sub4f_ref[t].astype(BF16)
        for r4 in range(4):
            for q4 in range(4):
                r16 = 4 * q4 + r4
                sub16_ref[t, r16 * n16:(r16 + 1) * n16, :] = (
                    sub4f_ref[t, pl.ds(r4 * n4 + q4, n16, stride=4), :].astype(BF16))

    def block(load, bias_ref, sub_len, r, i):
        base = r * sub_len
        kw = w if i == 0 else 2 * w
        krow = base + max(i - 1, 0) * w
        sel = min(i, 1)
        return _pair_block(load(0, base + i * w, w), load(1, krow, kw), load(2, krow, kw),
                           slopes, bias_ref[sel, 0:kw, :], low_half)

    dilated = ((0, 4, b4_ref, lambda t, row, size: sub4_ref[t, row:row + size, :]),
               (1, 16, b16_ref, lambda t, row, size: sub16_ref[t, row:row + size, :]))
    for pidx, dilation, bias_ref, load in dilated:
        sub_len = seq // dilation
        for r in range(dilation):
            for i in range(sub_len // w):
                num, den, m = block(load, bias_ref, sub_len, r, i)
                dst = pl.ds(dilation * w * i + r, w, stride=dilation)
                num_ref[pidx, dst, :] = num
                den_ref[pidx, dst, :] = den
                max_ref[pidx, dst, :] = m

    for i in range(seq // w):
        num1, den1, m1 = block(lambda t, row, size: srcs[t][row:row + size, :], b1_ref, seq, 0, i)
        rows = slice(i * w, (i + 1) * w)
        m2, m3 = max_ref[0, rows, :], max_ref[1, rows, :]
        mx = jnp.maximum(jnp.maximum(m1, m2), m3)
        e1, e2, e3 = jnp.exp(m1 - mx), jnp.exp(m2 - mx), jnp.exp(m3 - mx)
        num = e1 * num1 + e2 * num_ref[0, rows, :] + e3 * num_ref[1, rows, :]
        den = e1 * den1 + e2 * den_ref[0, rows, :] + e3 * den_ref[1, rows, :]
        o_ref[rows, :] = (num * (1.0 / den)).astype(BF16)


def _attention(q, k, v):
    bsz, seq, _ = q.shape
    w = ATT_BLOCK
    slopes = jnp.asarray(_alibi_slope_table(), BF16)
    biases = [jnp.asarray(_alibi_distance_table(d, min(2 * w, seq // d)), BF16) for _, d in PATTERNS]
    blk = pl.BlockSpec((None, seq, LANES), lambda b, p: (b, 0, p))
    return pl.pallas_call(
        functools.partial(_attn_body, seq=seq),
        grid=(bsz, N_HEADS // 2),
        in_specs=[blk, blk, blk, _const_spec(slopes.shape)] + [_const_spec(t.shape) for t in biases],
        out_specs=blk,
        out_shape=jax.ShapeDtypeStruct((bsz, seq, ATT_WIDTH), BF16),
        scratch_shapes=[pltpu.VMEM((3, seq, LANES), F32),
                        pltpu.VMEM((3, seq, LANES), F32),
                        pltpu.VMEM((3, seq, LANES), BF16),
                        pltpu.VMEM((3, seq, LANES), BF16),
                        pltpu.VMEM((2, seq, LANES), F32),
                        pltpu.VMEM((2, seq, LANES), F32),
                        pltpu.VMEM((2, seq, LANES), F32)],
        compiler_params=_params(2),
        name="attn",
    )(q, k, v, slopes, *biases)


def _placement(n_outer, n_inner):
    t = np.zeros((n_outer, n_inner, n_outer, SSM_GROUPS, n_inner), np.float32)
    for o in range(n_outer):
        for i in range(n_inner):
            t[o, i, o, :, i] = 1.0
    return t.reshape(n_outer * n_inner, n_outer * SSM_GROUPS * n_inner)


def _block_diagonal(val, n_outer, n_inner, row_inner):
    full = jnp.dot(val.astype(BF16), jnp.asarray(_placement(n_outer, n_inner)).astype(BF16),
                   preferred_element_type=F32)
    row_g = (lax.broadcasted_iota(jnp.int32, full.shape, 0) // row_inner) % SSM_GROUPS
    col_g = (lax.broadcasted_iota(jnp.int32, full.shape, 1) // n_inner) % SSM_GROUPS
    return jnp.where(row_g == col_g, full, 0.0).astype(BF16)


def _ssm_matrices(a_re, a_im, log_dt, b_re, b_im, c_re, c_im, d_skip):
    lc, g_n, n_st, ch = SSM_CHUNK, SSM_GROUPS, SSM_STATE, SSM_GROUP_CH
    hi = lax.Precision.HIGHEST
    lr, li = a_re.astype(F32), a_im.astype(F32)
    dt = jnp.exp(log_dt.astype(F32))[:, None]
    mag = jnp.exp(lr * dt)
    ang = li * dt
    ab_re, ab_im = mag * jnp.cos(ang), mag * jnp.sin(ang)
    nr, ni = ab_re - 1.0, ab_im
    den = lr * lr + li * li
    f_re = (nr * lr + ni * li) / den
    f_im = (ni * lr - nr * li) / den
    bb_re = f_re[..., None] * b_re - f_im[..., None] * b_im
    bb_im = f_re[..., None] * b_im + f_im[..., None] * b_re

    kk = jnp.arange(lc + 1, dtype=F32)[:, None, None]
    pw_mag = jnp.exp(kk * (lr * dt)[None])
    pw_re = pw_mag * jnp.cos(kk * ang[None])
    pw_im = pw_mag * jnp.sin(kk * ang[None])

    cp_re = c_re[None] * pw_re[:, :, None, :] - c_im[None] * pw_im[:, :, None, :]
    cp_im = c_re[None] * pw_im[:, :, None, :] + c_im[None] * pw_re[:, :, None, :]

    taps = (jnp.einsum('kgcn,gni->kgci', cp_re[:lc], bb_re, precision=hi)
            - jnp.einsum('kgcn,gni->kgci', cp_im[:lc], bb_im, precision=hi))
    skip = d_skip.reshape(g_n, ch)
    taps = taps.at[0].add(skip[:, :, None] * jnp.eye(ch, dtype=F32)[None])
    lag = np.arange(lc)[None, :] - np.arange(lc)[:, None]
    tz = taps[np.maximum(lag, 0)] * jnp.asarray(lag >= 0, F32)[:, :, None, None, None]
    tz = tz.transpose(0, 2, 4, 1, 3).reshape(lc * SSM_WIDTH, lc * ch)
    toeplitz = _block_diagonal(tz, lc, ch, ch)

    cv = jnp.stack([cp_re[1:], -cp_im[1:]], axis=0)
    cv = cv.transpose(0, 2, 4, 1, 3).reshape(STATE_COLS, lc * ch)
    carry = _block_diagonal(cv, lc, ch, n_st)

    rp_re, rp_im = pw_re[:lc][::-1], pw_im[:lc][::-1]
    bp_re = rp_re[..., None] * bb_re[None] - rp_im[..., None] * bb_im[None]
    bp_im = rp_re[..., None] * bb_im[None] + rp_im[..., None] * bb_re[None]
    bv = jnp.stack([bp_re, bp_im], axis=0).reshape(2, lc, g_n // 2, 2, n_st, ch)
    bst = jnp.einsum('pskinc,ij->kiscpjn', bv, jnp.eye(2, dtype=F32))
    bst = bst.reshape(g_n // 2, 2 * lc * ch, 4 * n_st).astype(BF16)

    half = g_n * n_st
    a_pow = jnp.concatenate([pw_re[lc].reshape(1, half), pw_im[lc].reshape(1, half)], axis=1)
    return bst, carry, toeplitz, a_pow


def _ssm_body(u_ref, ug_ref, bst_ref, wc_ref, wt_ref, apow_ref, wglu_ref, bglu_ref, o_ref,
              st_ref, y_ref, tok_ref, *, nb, n_chunks):
    lc = SSM_CHUNK
    half = STATE_COLS // 2
    rows = nb * n_chunks
    u2 = u_ref[...].reshape(rows, lc * SSM_WIDTH)
    ug = ug_ref[...].reshape(rows, lc * SSM_WIDTH)
    pw = 2 * SSM_STATE
    for k in range(SSM_GROUPS // 2):
        res = jnp.dot(ug[:, 2 * k * pw:2 * (k + 1) * pw], bst_ref[k], preferred_element_type=F32)
        st_ref[:, k * pw:(k + 1) * pw] = res[:, :pw]
        st_ref[:, half + k * pw:half + (k + 1) * pw] = res[:, pw:]

    a_r = apow_ref[:, 0:half]
    a_i = apow_ref[:, half:]

    hw = lc * SSM_WIDTH // 2
    for half_idx in range(2):
        cols = slice(half_idx * hw, (half_idx + 1) * hw)
        kdim = (half_idx + 1) * hw
        y_ref[:, cols] = jnp.dot(u2[:, :kdim], wt_ref[0:kdim, cols], preferred_element_type=F32)

    state = [jnp.zeros((1, half), F32) for _ in range(2 * nb)]
    for c in range(n_chunks):
        for b in range(nb):
            x_r, x_i = state[2 * b], state[2 * b + 1]
            row = slice(b * n_chunks + c, b * n_chunks + c + 1)
            s_r = st_ref[row, 0:half]
            s_i = st_ref[row, half:]
            st_ref[row, 0:half] = x_r
            st_ref[row, half:] = x_i
            state[2 * b] = a_r * x_r - a_i * x_i + s_r
            state[2 * b + 1] = a_r * x_i + a_i * x_r + s_i

    x_in = st_ref[...].astype(BF16)
    for half_idx in range(2):
        cols = slice(half_idx * hw, (half_idx + 1) * hw)
        y_ref[:, cols] += jnp.dot(x_in, wc_ref[:, cols], preferred_element_type=F32)
    for j in range(lc):
        y = jax.nn.gelu(y_ref[:, j * SSM_WIDTH:(j + 1) * SSM_WIDTH])
        gate = jnp.dot(y.astype(BF16), wglu_ref[...], preferred_element_type=F32) + bglu_ref[...]
        res = y * jax.nn.sigmoid(gate)
        for h in range(SSM_WIDTH // LANES):
            tok_ref[h, pl.ds(j, rows, stride=lc), :] = res[:, h * LANES:(h + 1) * LANES]
    out = jnp.concatenate([tok_ref[h] for h in range(SSM_WIDTH // LANES)], axis=1)
    o_ref[...] = out.astype(BF16).reshape(nb, n_chunks * lc, SSM_WIDTH)


def _ssm(u2, ug, bst, carry, toeplitz, a_pow, w_glu, b_glu, nb):
    bsz, n_chunks, _ = u2.shape
    lc = SSM_CHUNK
    seq = n_chunks * lc
    return pl.pallas_call(
        functools.partial(_ssm_body, nb=nb, n_chunks=n_chunks),
        grid=(bsz // nb,),
        in_specs=[pl.BlockSpec((nb, n_chunks, lc * SSM_WIDTH), lambda b: (b, 0, 0)),
                  pl.BlockSpec((nb, n_chunks, lc * SSM_WIDTH), lambda b: (b, 0, 0)),
                  _const_spec(bst.shape), _const_spec(carry.shape), _const_spec(toeplitz.shape),
                  _const_spec(a_pow.shape),
                  _const_spec((SSM_WIDTH, SSM_WIDTH)), _const_spec((1, SSM_WIDTH))],
        out_specs=pl.BlockSpec((nb, seq, SSM_WIDTH), lambda b: (b, 0, 0)),
        out_shape=jax.ShapeDtypeStruct((bsz, seq, SSM_WIDTH), BF16),
        scratch_shapes=[pltpu.VMEM((nb * n_chunks, STATE_COLS), F32),
                        pltpu.VMEM((nb * n_chunks, lc * SSM_WIDTH), F32),
                        pltpu.VMEM((SSM_WIDTH // LANES, nb * seq, LANES), F32)],
        compiler_params=_params(1),
        name="ssm",
    )(u2, ug, bst, carry, toeplitz, a_pow, w_glu.astype(BF16), b_glu.reshape(1, SSM_WIDTH))


def _tail_body(x_ref, oatt_ref, sg_ref, sa_ref, ss_ref,
               mod_ref, wpa_ref, wps_ref, wo_ref, gf_ref, wup_ref, wc_ref, bc_ref,
               wdn_ref, gfin_ref, out_ref, carry_ref):
    tm = x_ref.shape[0]

    @pl.when(pl.program_id(1) == 0)
    def _():
        carry_ref[...] = jnp.zeros_like(carry_ref)

    y_att = jnp.dot(oatt_ref[...], wpa_ref[...], preferred_element_type=F32)
    y_ssm = jnp.dot(sg_ref[...], wps_ref[...], preferred_element_type=F32)
    merged = sa_ref[...].astype(F32) * y_att + ss_ref[...].astype(F32) * y_ssm
    mix = jnp.dot(merged.astype(BF16), wo_ref[...], preferred_element_type=F32)
    h1 = x_ref[...] + mod_ref[2:3, :] * mix

    u = _rms_modulate(h1, gf_ref[...], mod_ref[3:4, :], mod_ref[4:5, :]).astype(BF16)
    a = jnp.dot(u, wup_ref[:, 0:D_FF], preferred_element_type=F32)
    val = jnp.dot(u, wup_ref[:, D_FF:], preferred_element_type=F32)

    row = lax.broadcasted_iota(jnp.int32, a.shape, 0)
    prev1 = carry_ref[SUBLANES - 1:SUBLANES, :]
    prev2 = carry_ref[SUBLANES - 2:SUBLANES - 1, :]
    a1 = jnp.where(row == 0, prev1, pltpu.roll(a, 1, 0))
    a2 = jnp.where(row == 0, prev2, jnp.where(row == 1, prev1, pltpu.roll(a, 2, 0)))
    carry_ref[...] = a[tm - SUBLANES:, :]
    conv = bc_ref[...] + wc_ref[0:1, :] * a + wc_ref[1:2, :] * a1 + wc_ref[2:3, :] * a2
    act = (conv * jax.nn.sigmoid(conv) * val).astype(BF16)
    ffn = jnp.dot(act, wdn_ref[...], preferred_element_type=F32)
    h2 = h1 + mod_ref[5:6, :] * ffn

    ms = jnp.mean(h2 * h2, axis=-1, keepdims=True)
    out_ref[...] = (h2 * lax.rsqrt(ms + EPS)) * gfin_ref[...]


def _tail(x, o_att, sglu, sig_att, sig_ssm, mod3, w_proj_att, w_proj_ssm, w_out,
          g_ffn, w_up, w_conv, b_conv, w_down, g_final, tm):
    bsz, seq, d = x.shape

    def tok(width):
        return pl.BlockSpec((None, tm, width), lambda b, s: (b, s, 0))

    in_specs = [tok(d), tok(ATT_WIDTH), tok(SSM_WIDTH), tok(d), tok(d),
                pl.BlockSpec((None, 6, d), lambda b, s: (b, 0, 0)),
                _const_spec((ATT_WIDTH, d)), _const_spec((SSM_WIDTH, d)), _const_spec((d, d)),
                _const_spec((1, d)), _const_spec((d, 2 * D_FF)), _const_spec((CONV_W, D_FF)),
                _const_spec((1, D_FF)), _const_spec((D_FF, d)), _const_spec((1, d))]
    return pl.pallas_call(
        _tail_body,
        grid=(bsz, seq // tm),
        in_specs=in_specs,
        out_specs=tok(d),
        out_shape=jax.ShapeDtypeStruct((bsz, seq, d), F32),
        scratch_shapes=[pltpu.VMEM((SUBLANES, D_FF), F32)],
        compiler_params=_params(2, TAIL_VMEM_LIMIT_BYTES),
        name="tail",
    )(x, o_att, sglu, sig_att, sig_ssm, mod3,
      w_proj_att.astype(BF16), w_proj_ssm.astype(BF16), w_out.astype(BF16),
      g_ffn.reshape(1, d), w_up.astype(BF16), w_conv, b_conv.reshape(1, D_FF),
      w_down.astype(BF16), g_final.reshape(1, d))


def kernel(x, c, w_ada, b_ada, g_mix, w_in, b_gate, a_re, a_im, log_dt, b_re, b_im, c_re, c_im,
           d_skip, w_glu, b_glu, w_proj_att, w_proj_ssm, w_out, g_ffn, w_up, w_conv, b_conv,
           w_down, g_final):
    depth = w_ada.shape[0]
    assert depth == 1, "the final RMSNorm is fused into the single layer's tail kernel"
    bsz, seq, d = x.shape
    l = 0
    mod3 = _ada(c, w_ada[l], b_ada[l]).reshape(bsz, 6, d)
    q, k, v, us, ug, sig_att, sig_ssm = _inproj(x, mod3, g_mix[l], w_in[l], b_gate[l], tm=1024)

    o_att = _attention(q, k, v)

    ssm_mats = _ssm_matrices(a_re[l], a_im[l], log_dt[l], b_re[l], b_im[l], c_re[l], c_im[l],
                             d_skip[l])
    sglu = _ssm(us, ug, *ssm_mats, w_glu[l], b_glu[l], nb=2)

    return _tail(x, o_att, sglu, sig_att, sig_ssm, mod3, w_proj_att[l], w_proj_ssm[l],
                 w_out[l], g_ffn[l], w_up[l], w_conv[l], b_conv[l], w_down[l], g_final, tm=1024)
```

```python
import functools
import math

import jax
import jax.numpy as jnp
import numpy as np
from jax import lax
from jax.experimental import pallas as pl
from jax.experimental.pallas import tpu as pltpu

D_MODEL = 1024
N_HEADS = 8
HEAD_DIM = 64
ATT_WIDTH = N_HEADS * HEAD_DIM
PATTERNS = ((128, 1), (512, 4), (2048, 16))
SSM_GROUPS = 16
SSM_GROUP_CH = 16
SSM_WIDTH = SSM_GROUPS * SSM_GROUP_CH
SSM_STATE = 64
D_FF = 2048
CONV_W = 3
EPS = 1e-6
NEG_INF = -1e30
LOG2E = math.log2(math.e)

LANES = 128
SUBLANES = 8
VMEM_BYTES = 64 * 1024 * 1024
VMEM_LIMIT_BYTES = 56 * 1024 * 1024
TAIL_VMEM_LIMIT_BYTES = VMEM_BYTES - 4 * 1024 * 1024

ATT_BLOCK = 128
SSM_CHUNK = 8
INPROJ_ROW_SLICES = 8
TAIL_MIX_ROW_SLICES = 4
STATE_COLS = 2 * SSM_GROUPS * SSM_STATE

BF16 = jnp.bfloat16
F32 = jnp.float32


def _const_spec(shape):
    zeros = (0,) * len(shape)
    return pl.BlockSpec(shape, lambda *_: zeros, pipeline_mode=pl.Buffered(1))


def _params(n_axes, vmem_limit_bytes=VMEM_LIMIT_BYTES):
    return pltpu.CompilerParams(
        dimension_semantics=("arbitrary",) * n_axes,
        vmem_limit_bytes=vmem_limit_bytes)


def _ada_body(c_ref, w_ref, b_ref, o_ref):
    c = c_ref[...]
    act = (c * jax.nn.sigmoid(c)).astype(BF16)
    o_ref[...] = jnp.dot(act, w_ref[...], preferred_element_type=F32) + b_ref[...]


def _ada(c, w_ada, b_ada):
    bsz = c.shape[0]
    n_out = w_ada.shape[1]
    tn = 1536
    return pl.pallas_call(
        _ada_body,
        grid=(n_out // tn,),
        in_specs=[_const_spec((bsz, D_MODEL)),
                  pl.BlockSpec((D_MODEL, tn), lambda j: (0, j)),
                  pl.BlockSpec((1, tn), lambda j: (0, j))],
        out_specs=pl.BlockSpec((bsz, tn), lambda j: (0, j)),
        out_shape=jax.ShapeDtypeStruct((bsz, n_out), F32),
        compiler_params=_params(1),
        name="ada",
    )(c, w_ada.astype(BF16), b_ada.reshape(1, n_out))


def _rms_modulate(x, gain, shift, scale):
    ms = jnp.mean(x * x, axis=-1, keepdims=True)
    return (x * lax.rsqrt(ms + EPS)) * (gain * (1.0 + scale)) + shift


def _inproj_body(x_ref, mod_ref, g_ref, w_ref, bg_ref,
                 q_ref, k_ref, v_ref, us_ref, ug_ref, sa_ref, ss_ref, us_scr):
    tm = x_ref.shape[0]
    hm = tm // INPROJ_ROW_SLICES
    a = ATT_WIDTH
    n_tiles = SSM_WIDTH // LANES
    for part in range(INPROJ_ROW_SLICES):
        rows = slice(part * hm, (part + 1) * hm)
        u = _rms_modulate(x_ref[rows, :], g_ref[...], mod_ref[0:1, :], mod_ref[1:2, :]).astype(BF16)

        def proj(lo, hi):
            return jnp.dot(u, w_ref[:, lo:hi], preferred_element_type=F32)

        q_ref[rows, :] = (proj(0, a) * (HEAD_DIM ** -0.5)).astype(BF16)
        k_ref[rows, :] = proj(a, 2 * a).astype(BF16)
        v_ref[rows, :] = proj(2 * a, 3 * a).astype(BF16)
        o = 3 * a
        us = proj(o, o + SSM_WIDTH)
        for h in range(n_tiles):
            us_scr[h, rows, :] = us[:, h * LANES:(h + 1) * LANES]
        o += SSM_WIDTH
        sa_ref[rows, :] = jax.nn.sigmoid(proj(o, o + D_MODEL) + bg_ref[:, 0:D_MODEL]).astype(BF16)
        o += D_MODEL
        ss_ref[rows, :] = jax.nn.sigmoid(proj(o, o + D_MODEL) + bg_ref[:, D_MODEL:]).astype(BF16)
    n_rows = tm // SSM_CHUNK
    pieces = [[us_scr[h, pl.ds(s, n_rows, stride=SSM_CHUNK), :] for h in range(n_tiles)]
              for s in range(SSM_CHUNK)]
    us_ref[...] = jnp.concatenate(
        [pieces[s][h] for s in range(SSM_CHUNK) for h in range(n_tiles)], axis=1).astype(BF16)
    per_tile = LANES // SSM_GROUP_CH
    ug_ref[...] = jnp.concatenate(
        [pieces[s][g // per_tile][:, (g % per_tile) * SSM_GROUP_CH:(g % per_tile + 1) * SSM_GROUP_CH]
         for g in range(SSM_GROUPS) for s in range(SSM_CHUNK)], axis=1).astype(BF16)


def _inproj(x, mod3, g_mix, w_in, b_gate, tm):
    bsz, seq, d = x.shape
    in_width = w_in.shape[1]

    def tok(width):
        return pl.BlockSpec((None, tm, width), lambda b, s: (b, s, 0))

    def out(width):
        return jax.ShapeDtypeStruct((bsz, seq, width), BF16)

    return pl.pallas_call(
        _inproj_body,
        grid=(bsz, seq // tm),
        in_specs=[tok(d),
                  pl.BlockSpec((None, 6, d), lambda b, s: (b, 0, 0)),
                  _const_spec((1, d)),
                  _const_spec((d, in_width)),
                  _const_spec((1, 2 * d))],
        out_specs=[tok(ATT_WIDTH), tok(ATT_WIDTH), tok(ATT_WIDTH),
                   pl.BlockSpec((None, tm // SSM_CHUNK, SSM_CHUNK * SSM_WIDTH), lambda b, s: (b, s, 0)),
                   pl.BlockSpec((None, tm // SSM_CHUNK, SSM_CHUNK * SSM_WIDTH), lambda b, s: (b, s, 0)),
                   tok(d), tok(d)],
        out_shape=[out(ATT_WIDTH), out(ATT_WIDTH), out(ATT_WIDTH),
                   jax.ShapeDtypeStruct((bsz, seq // SSM_CHUNK, SSM_CHUNK * SSM_WIDTH), BF16),
                   jax.ShapeDtypeStruct((bsz, seq // SSM_CHUNK, SSM_CHUNK * SSM_WIDTH), BF16),
                   out(d), out(d)],
        scratch_shapes=[pltpu.VMEM((SSM_WIDTH // LANES, tm, LANES), F32)],
        compiler_params=_params(2),
        name="inproj",
    )(x, mod3, g_mix.reshape(1, d), w_in.astype(BF16), b_gate.reshape(1, 2 * d))


def _alibi_distance_table(dilation, kw):
    w = ATT_BLOCK
    j = np.arange(kw)[:, None]
    a = np.arange(w)[None, :]
    tabs = []
    for sel in range(2):
        dist = (sel * w + a - j).astype(np.float32)
        valid = (dist >= 0) & (dist <= w)
        tabs.append(np.where(valid, -dilation * dist, NEG_INF))
    return np.stack(tabs).astype(np.float32)


def _alibi_slope_table():
    w = ATT_BLOCK
    tabs = np.zeros((N_HEADS // 2, 2 * w, w), np.float32)
    for h in range(N_HEADS):
        slope = 2.0 ** (-8.0 * (h + 1) / N_HEADS)
        tabs[h // 2, (h % 2) * w:(h % 2 + 1) * w, :] = slope * np.eye(w, dtype=np.float32)
    return tabs


def _pair_block(qp, kp, vp, slopes, dist_t, low_half):
    w = ATT_BLOCK
    kw = kp.shape[0]
    zero = jnp.zeros((), BF16)
    qs = jnp.concatenate([jnp.where(low_half, qp, zero), jnp.where(low_half, zero, qp)], axis=0)
    lhs = jnp.concatenate([qs, slopes], axis=1)
    rhs = jnp.concatenate([kp, dist_t], axis=1)
    s2 = lax.dot_general(lhs, rhs, (((1,), (1,)), ((), ())), preferred_element_type=F32)
    v_ones = jnp.concatenate([vp, jnp.ones((kw, LANES), BF16)], axis=1)
    res, maxes = [], []
    for hh in range(2):
        s = s2[hh * w:(hh + 1) * w]
        m = jnp.max(s, axis=-1, keepdims=True)
        p = jnp.exp2(((s - m) * LOG2E).astype(BF16))
        res.append(jnp.dot(p, v_ones, preferred_element_type=F32))
        maxes.append(m)
    num = jnp.where(low_half, res[0][:, :LANES], res[1][:, :LANES])
    den = jnp.where(low_half, res[0][:, LANES:], res[1][:, LANES:])
    return num, den, jnp.where(low_half, maxes[0], maxes[1])


def _attn_body(q_ref, k_ref, v_ref, slope_ref, b1_ref, b4_ref, b16_ref, o_ref,
               f32_ref, sub4f_ref, sub4_ref, sub16_ref, num_ref, den_ref, max_ref, *, seq):
    w = ATT_BLOCK
    pair = pl.program_id(1)
    low_half = lax.broadcasted_iota(jnp.int32, (w, LANES), 1) < HEAD_DIM
    slopes = slope_ref[pair]
    srcs = (q_ref, k_ref, v_ref)
    n4, n16 = seq // 4, seq // 16
    for t in range(3):
        f32_ref[t] = srcs[t][...].astype(F32)
    for t in range(3):
        for r4 in range(4):
            sub4f_ref[t, r4 * n4:(r4 + 1) * n4, :] = f32_ref[t, pl.ds(r4, n4, stride=4), :]
    for t in range(3):
        sub4_ref[t] = sub4f_ref[t].astype(BF16)
        for r4 in range(4):
            for q4 in range(4):
                r16 = 4 * q4 + r4
                sub16_ref[t, r16 * n16:(r16 + 1) * n16, :] = (
                    sub4f_ref[t, pl.ds(r4 * n4 + q4, n16, stride=4), :].astype(BF16))

    def block(load, bias_ref, sub_len, r, i):
        base = r * sub_len
        kw = w if i == 0 else 2 * w
        krow = base + max(i - 1, 0) * w
        sel = min(i, 1)
        return _pair_block(load(0, base + i * w, w), load(1, krow, kw), load(2, krow, kw),
                           slopes, bias_ref[sel, 0:kw, :], low_half)

    dilated = ((0, 4, b4_ref, lambda t, row, size: sub4_ref[t, row:row + size, :]),
               (1, 16, b16_ref, lambda t, row, size: sub16_ref[t, row:row + size, :]))
    for pidx, dilation, bias_ref, load in dilated:
        sub_len = seq // dilation
        for r in range(dilation):
            for i in range(sub_len // w):
                num, den, m = block(load, bias_ref, sub_len, r, i)
                dst = pl.ds(dilation * w * i + r, w, stride=dilation)
                num_ref[pidx, dst, :] = num
                den_ref[pidx, dst, :] = den
                max_ref[pidx, dst, :] = m

    for i in range(seq // w):
        num1, den1, m1 = block(lambda t, row, size: srcs[t][row:row + size, :], b1_ref, seq, 0, i)
        rows = slice(i * w, (i + 1) * w)
        m2, m3 = max_ref[0, rows, :], max_ref[1, rows, :]
        mx = jnp.maximum(jnp.maximum(m1, m2), m3)
        e1, e2, e3 = jnp.exp(m1 - mx), jnp.exp(m2 - mx), jnp.exp(m3 - mx)
        num = e1 * num1 + e2 * num_ref[0, rows, :] + e3 * num_ref[1, rows, :]
        den = e1 * den1 + e2 * den_ref[0, rows, :] + e3 * den_ref[1, rows, :]
        o_ref[rows, :] = (num * (1.0 / den)).astype(BF16)


def _attention(q, k, v):
    bsz, seq, _ = q.shape
    w = ATT_BLOCK
    slopes = jnp.asarray(_alibi_slope_table(), BF16)
    biases = [jnp.asarray(_alibi_distance_table(d, min(2 * w, seq // d)), BF16) for _, d in PATTERNS]
    blk = pl.BlockSpec((None, seq, LANES), lambda b, p: (b, 0, p))
    return pl.pallas_call(
        functools.partial(_attn_body, seq=seq),
        grid=(bsz, N_HEADS // 2),
        in_specs=[blk, blk, blk, _const_spec(slopes.shape)] + [_const_spec(t.shape) for t in biases],
        out_specs=blk,
        out_shape=jax.ShapeDtypeStruct((bsz, seq, ATT_WIDTH), BF16),
        scratch_shapes=[pltpu.VMEM((3, seq, LANES), F32),
                        pltpu.VMEM((3, seq, LANES), F32),
                        pltpu.VMEM((3, seq, LANES), BF16),
                        pltpu.VMEM((3, seq, LANES), BF16),
                        pltpu.VMEM((2, seq, LANES), F32),
                        pltpu.VMEM((2, seq, LANES), F32),
                        pltpu.VMEM((2, seq, LANES), F32)],
        compiler_params=_params(2),
        name="attn",
    )(q, k, v, slopes, *biases)


def _placement(n_outer, n_inner):
    t = np.zeros((n_outer, n_inner, n_outer, SSM_GROUPS, n_inner), np.float32)
    for o in range(n_outer):
        for i in range(n_inner):
            t[o, i, o, :, i] = 1.0
    return t.reshape(n_outer * n_inner, n_outer * SSM_GROUPS * n_inner)


def _block_diagonal(val, n_outer, n_inner, row_inner):
    full = jnp.dot(val.astype(BF16), jnp.asarray(_placement(n_outer, n_inner)).astype(BF16),
                   preferred_element_type=F32)
    row_g = (lax.broadcasted_iota(jnp.int32, full.shape, 0) // row_inner) % SSM_GROUPS
    col_g = (lax.broadcasted_iota(jnp.int32, full.shape, 1) // n_inner) % SSM_GROUPS
    return jnp.where(row_g == col_g, full, 0.0).astype(BF16)


def _ssm_matrices(a_re, a_im, log_dt, b_re, b_im, c_re, c_im, d_skip):
    lc, g_n, n_st, ch = SSM_CHUNK, SSM_GROUPS, SSM_STATE, SSM_GROUP_CH
    hi = lax.Precision.HIGHEST
    lr, li = a_re.astype(F32), a_im.astype(F32)
    dt = jnp.exp(log_dt.astype(F32))[:, None]
    mag = jnp.exp(lr * dt)
    ang = li * dt
    ab_re, ab_im = mag * jnp.cos(ang), mag * jnp.sin(ang)
    nr, ni = ab_re - 1.0, ab_im
    den = lr * lr + li * li
    f_re = (nr * lr + ni * li) / den
    f_im = (ni * lr - nr * li) / den
    bb_re = f_re[..., None] * b_re - f_im[..., None] * b_im
    bb_im = f_re[..., None] * b_im + f_im[..., None] * b_re

    kk = jnp.arange(lc + 1, dtype=F32)[:, None, None]
    pw_mag = jnp.exp(kk * (lr * dt)[None])
    pw_re = pw_mag * jnp.cos(kk * ang[None])
    pw_im = pw_mag * jnp.sin(kk * ang[None])

    cp_re = c_re[None] * pw_re[:, :, None, :] - c_im[None] * pw_im[:, :, None, :]
    cp_im = c_re[None] * pw_im[:, :, None, :] + c_im[None] * pw_re[:, :, None, :]

    taps = (jnp.einsum('kgcn,gni->kgci', cp_re[:lc], bb_re, precision=hi)
            - jnp.einsum('kgcn,gni->kgci', cp_im[:lc], bb_im, precision=hi))
    skip = d_skip.reshape(g_n, ch)
    taps = taps.at[0].add(skip[:, :, None] * jnp.eye(ch, dtype=F32)[None])
    lag = np.arange(lc)[None, :] - np.arange(lc)[:, None]
    tz = taps[np.maximum(lag, 0)] * jnp.asarray(lag >= 0, F32)[:, :, None, None, None]
    tz = tz.transpose(0, 2, 4, 1, 3).reshape(lc * SSM_WIDTH, lc * ch)
    toeplitz = _block_diagonal(tz, lc, ch, ch)

    cv = jnp.stack([cp_re[1:], -cp_im[1:]], axis=0)
    cv = cv.transpose(0, 2, 4, 1, 3).reshape(STATE_COLS, lc * ch)
    carry = _block_diagonal(cv, lc, ch, n_st)

    rp_re, rp_im = pw_re[:lc][::-1], pw_im[:lc][::-1]
    bp_re = rp_re[..., None] * bb_re[None] - rp_im[..., None] * bb_im[None]
    bp_im = rp_re[..., None] * bb_im[None] + rp_im[..., None] * bb_re[None]
    bv = jnp.stack([bp_re, bp_im], axis=0).reshape(2, lc, g_n // 2, 2, n_st, ch)
    bst = jnp.einsum('pskinc,ij->kiscpjn', bv, jnp.eye(2, dtype=F32))
    bst = bst.reshape(g_n // 2, 2 * lc * ch, 4 * n_st).astype(BF16)

    half = g_n * n_st
    a_pow = jnp.concatenate([pw_re[lc].reshape(1, half), pw_im[lc].reshape(1, half)], axis=1)
    return bst, carry, toeplitz, a_pow


def _ssm_body(u_ref, ug_ref, bst_ref, wc_ref, wt_ref, apow_ref, wglu_ref, bglu_ref, o_ref,
              st_ref, y_ref, tok_ref, *, nb, n_chunks):
    lc = SSM_CHUNK
    half = STATE_COLS // 2
    rows = nb * n_chunks
    u2 = u_ref[...].reshape(rows, lc * SSM_WIDTH)
    ug = ug_ref[...].reshape(rows, lc * SSM_WIDTH)
    pw = 2 * SSM_STATE
    for k in range(SSM_GROUPS // 2):
        res = jnp.dot(ug[:, 2 * k * pw:2 * (k + 1) * pw], bst_ref[k], preferred_element_type=F32)
        st_ref[:, k * pw:(k + 1) * pw] = res[:, :pw]
        st_ref[:, half + k * pw:half + (k + 1) * pw] = res[:, pw:]

    a_r = apow_ref[:, 0:half]
    a_i = apow_ref[:, half:]

    hw = lc * SSM_WIDTH // 2
    for half_idx in range(2):
        cols = slice(half_idx * hw, (half_idx + 1) * hw)
        kdim = (half_idx + 1) * hw
        y_ref[:, cols] = jnp.dot(u2[:, :kdim], wt_ref[0:kdim, cols], preferred_element_type=F32)

    state = [jnp.zeros((1, half), F32) for _ in range(2 * nb)]
    for c in range(n_chunks):
        for b in range(nb):
            x_r, x_i = state[2 * b], state[2 * b + 1]
            row = slice(b * n_chunks + c, b * n_chunks + c + 1)
            s_r = st_ref[row, 0:half]
            s_i = st_ref[row, half:]
            st_ref[row, 0:half] = x_r
            st_ref[row, half:] = x_i
            state[2 * b] = a_r * x_r - a_i * x_i + s_r
            state[2 * b + 1] = a_r * x_i + a_i * x_r + s_i

    x_in = st_ref[...].astype(BF16)
    for half_idx in range(2):
        cols = slice(half_idx * hw, (half_idx + 1) * hw)
        y_ref[:, cols] += jnp.dot(x_in, wc_ref[:, cols], preferred_element_type=F32)
    for j in range(lc):
        y = jax.nn.gelu(y_ref[:, j * SSM_WIDTH:(j + 1) * SSM_WIDTH])
        gate = jnp.dot(y.astype(BF16), wglu_ref[...], preferred_element_type=F32) + bglu_ref[...]
        res = y * jax.nn.sigmoid(gate)
        for h in range(SSM_WIDTH // LANES):
            tok_ref[h, pl.ds(j, rows, stride=lc), :] = res[:, h * LANES:(h + 1) * LANES]
    out = jnp.concatenate([tok_ref[h] for h in range(SSM_WIDTH // LANES)], axis=1)
    o_ref[...] = out.astype(BF16).reshape(nb, n_chunks * lc, SSM_WIDTH)


def _ssm(u2, ug, bst, carry, toeplitz, a_pow, w_glu, b_glu, nb):
    bsz, n_chunks, _ = u2.shape
    lc = SSM_CHUNK
    seq = n_chunks * lc
    return pl.pallas_call(
        functools.partial(_ssm_body, nb=nb, n_chunks=n_chunks),
        grid=(bsz // nb,),
        in_specs=[pl.BlockSpec((nb, n_chunks, lc * SSM_WIDTH), lambda b: (b, 0, 0)),
                  pl.BlockSpec((nb, n_chunks, lc * SSM_WIDTH), lambda b: (b, 0, 0)),
                  _const_spec(bst.shape), _const_spec(carry.shape), _const_spec(toeplitz.shape),
                  _const_spec(a_pow.shape),
                  _const_spec((SSM_WIDTH, SSM_WIDTH)), _const_spec((1, SSM_WIDTH))],
        out_specs=pl.BlockSpec((nb, seq, SSM_WIDTH), lambda b: (b, 0, 0)),
        out_shape=jax.ShapeDtypeStruct((bsz, seq, SSM_WIDTH), BF16),
        scratch_shapes=[pltpu.VMEM((nb * n_chunks, STATE_COLS), F32),
                        pltpu.VMEM((nb * n_chunks, lc * SSM_WIDTH), F32),
                        pltpu.VMEM((SSM_WIDTH // LANES, nb * seq, LANES), F32)],
        compiler_params=_params(1),
        name="ssm",
    )(u2, ug, bst, carry, toeplitz, a_pow, w_glu.astype(BF16), b_glu.reshape(1, SSM_WIDTH))


def _tail_body(x_ref, oatt_ref, sg_ref, sa_ref, ss_ref,
               mod_ref, wpa_ref, wps_ref, wo_ref, gf_ref, wup_ref, wc_ref, bc_ref,
               wdn_ref, gfin_ref, out_ref, carry_ref):
    tm = x_ref.shape[0]

    @pl.when(pl.program_id(1) == 0)
    def _():
        carry_ref[...] = jnp.zeros_like(carry_ref)

    hm = tm // TAIL_MIX_ROW_SLICES
    h1_parts = []
    for part in range(TAIL_MIX_ROW_SLICES):
        rows = slice(part * hm, (part + 1) * hm)
        y_att = jnp.dot(oatt_ref[rows, :], wpa_ref[...], preferred_element_type=F32)
        y_ssm = jnp.dot(sg_ref[rows, :], wps_ref[...], preferred_element_type=F32)
        merged = sa_ref[rows, :].astype(F32) * y_att + ss_ref[rows, :].astype(F32) * y_ssm
        mix = jnp.dot(merged.astype(BF16), wo_ref[...], preferred_element_type=F32)
        h1_parts.append(x_ref[rows, :] + mod_ref[2:3, :] * mix)
    h1 = jnp.concatenate(h1_parts, axis=0)

    u = _rms_modulate(h1, gf_ref[...], mod_ref[3:4, :], mod_ref[4:5, :]).astype(BF16)
    a = jnp.dot(u, wup_ref[:, 0:D_FF], preferred_element_type=F32)
    val = jnp.dot(u, wup_ref[:, D_FF:], preferred_element_type=F32)

    row = lax.broadcasted_iota(jnp.int32, a.shape, 0)
    prev1 = carry_ref[SUBLANES - 1:SUBLANES, :]
    prev2 = carry_ref[SUBLANES - 2:SUBLANES - 1, :]
    a1 = jnp.where(row == 0, prev1, pltpu.roll(a, 1, 0))
    a2 = jnp.where(row == 0, prev2, jnp.where(row == 1, prev1, pltpu.roll(a, 2, 0)))
    carry_ref[...] = a[tm - SUBLANES:, :]
    conv = bc_ref[...] + wc_ref[0:1, :] * a + wc_ref[1:2, :] * a1 + wc_ref[2:3, :] * a2
    act = (conv * jax.nn.sigmoid(conv) * val).astype(BF16)
    ffn = jnp.dot(act, wdn_ref[...], preferred_element_type=F32)
    h2 = h1 + mod_ref[5:6, :] * ffn

    ms = jnp.mean(h2 * h2, axis=-1, keepdims=True)
    out_ref[...] = (h2 * lax.rsqrt(ms + EPS)) * gfin_ref[...]


def _tail(x, o_att, sglu, sig_att, sig_ssm, mod3, w_proj_att, w_proj_ssm, w_out,
          g_ffn, w_up, w_conv, b_conv, w_down, g_final, tm):
    bsz, seq, d = x.shape

    def tok(width):
        return pl.BlockSpec((None, tm, width), lambda b, s: (b, s, 0))

    in_specs = [tok(d), tok(ATT_WIDTH), tok(SSM_WIDTH), tok(d), tok(d),
                pl.BlockSpec((None, 6, d), lambda b, s: (b, 0, 0)),
                _const_spec((ATT_WIDTH, d)), _const_spec((SSM_WIDTH, d)), _const_spec((d, d)),
                _const_spec((1, d)), _const_spec((d, 2 * D_FF)), _const_spec((CONV_W, D_FF)),
                _const_spec((1, D_FF)), _const_spec((D_FF, d)), _const_spec((1, d))]
    return pl.pallas_call(
        _tail_body,
        grid=(bsz, seq // tm),
        in_specs=in_specs,
        out_specs=tok(d),
        out_shape=jax.ShapeDtypeStruct((bsz, seq, d), F32),
        scratch_shapes=[pltpu.VMEM((SUBLANES, D_FF), F32)],
        compiler_params=_params(2, TAIL_VMEM_LIMIT_BYTES),
        name="tail",
    )(x, o_att, sglu, sig_att, sig_ssm, mod3,
      w_proj_att.astype(BF16), w_proj_ssm.astype(BF16), w_out.astype(BF16),
      g_ffn.reshape(1, d), w_up.astype(BF16), w_conv, b_conv.reshape(1, D_FF),
      w_down.astype(BF16), g_final.reshape(1, d))


def kernel(x, c, w_ada, b_ada, g_mix, w_in, b_gate, a_re, a_im, log_dt, b_re, b_im, c_re, c_im,
           d_skip, w_glu, b_glu, w_proj_att, w_proj_ssm, w_out, g_ffn, w_up, w_conv, b_conv,
           w_down, g_final):
    depth = w_ada.shape[0]
    assert depth == 1, "the final RMSNorm is fused into the single layer's tail kernel"
    bsz, seq, d = x.shape
    l = 0
    mod3 = _ada(c, w_ada[l], b_ada[l]).reshape(bsz, 6, d)
    q, k, v, us, ug, sig_att, sig_ssm = _inproj(x, mod3, g_mix[l], w_in[l], b_gate[l], tm=1024)

    o_att = _attention(q, k, v)

    ssm_mats = _ssm_matrices(a_re[l], a_im[l], log_dt[l], b_re[l], b_im[l], c_re[l], c_im[l],
                             d_skip[l])
    sglu = _ssm(us, ug, *ssm_mats, w_glu[l], b_glu[l], nb=2)

    return _tail(x, o_att, sglu, sig_att, sig_ssm, mod3, w_proj_att[l], w_proj_ssm[l],
                 w_out[l], g_ffn[l], w_up[l], w_conv[l], b_conv[l], w_down[l], g_final, tm=1024)
```

```python
import functools
import math

import jax
import jax.numpy as jnp
import numpy as np
from jax import lax
from jax.experimental import pallas as pl
from jax.experimental.pallas import tpu as pltpu

D_MODEL = 1024
N_HEADS = 8
HEAD_DIM = 64
ATT_WIDTH = N_HEADS * HEAD_DIM
PATTERNS = ((128, 1), (512, 4), (2048, 16))
SSM_GROUPS = 16
SSM_GROUP_CH = 16
SSM_WIDTH = SSM_GROUPS * SSM_GROUP_CH
SSM_STATE = 64
D_FF = 2048
CONV_W = 3
EPS = 1e-6
NEG_INF = -1e30
LOG2E = math.log2(math.e)

LANES = 128
SUBLANES = 8
VMEM_BYTES = 64 * 1024 * 1024
VMEM_LIMIT_BYTES = 56 * 1024 * 1024
TAIL_VMEM_LIMIT_BYTES = VMEM_BYTES - 4 * 1024 * 1024

ATT_BLOCK = 128
SSM_CHUNK = 8
INPROJ_ROW_SLICES = 8
TAIL_MIX_ROW_SLICES = 4
STATE_COLS = 2 * SSM_GROUPS * SSM_STATE

BF16 = jnp.bfloat16
F32 = jnp.float32


def _const_spec(shape):
    zeros = (0,) * len(shape)
    return pl.BlockSpec(shape, lambda *_: zeros, pipeline_mode=pl.Buffered(1))


def _params(n_axes, vmem_limit_bytes=VMEM_LIMIT_BYTES):
    return pltpu.CompilerParams(
        dimension_semantics=("arbitrary",) * n_axes,
        vmem_limit_bytes=vmem_limit_bytes)


def _ada_body(c_ref, w_ref, b_ref, o_ref):
    c = c_ref[...]
    act = (c * jax.nn.sigmoid(c)).astype(BF16)
    o_ref[...] = jnp.dot(act, w_ref[...], preferred_element_type=F32) + b_ref[...]


def _ada(c, w_ada, b_ada):
    bsz = c.shape[0]
    n_out = w_ada.shape[1]
    tn = 1536
    return pl.pallas_call(
        _ada_body,
        grid=(n_out // tn,),
        in_specs=[_const_spec((bsz, D_MODEL)),
                  pl.BlockSpec((D_MODEL, tn), lambda j: (0, j)),
                  pl.BlockSpec((1, tn), lambda j: (0, j))],
        out_specs=pl.BlockSpec((bsz, tn), lambda j: (0, j)),
        out_shape=jax.ShapeDtypeStruct((bsz, n_out), F32),
        compiler_params=_params(1),
        name="ada",
    )(c, w_ada.astype(BF16), b_ada.reshape(1, n_out))


def _rms_modulate(x, gain, shift, scale):
    ms = jnp.mean(x * x, axis=-1, keepdims=True)
    return (x * lax.rsqrt(ms + EPS)) * (gain * (1.0 + scale)) + shift


def _inproj_body(x_ref, mod_ref, g_ref, w_ref, bg_ref,
                 q_ref, k_ref, v_ref, us_ref, ug_ref, sa_ref, ss_ref, us_scr):
    tm = x_ref.shape[0]
    hm = tm // INPROJ_ROW_SLICES
    a = ATT_WIDTH
    n_tiles = SSM_WIDTH // LANES
    for part in range(INPROJ_ROW_SLICES):
        rows = slice(part * hm, (part + 1) * hm)
        u = _rms_modulate(x_ref[rows, :], g_ref[...], mod_ref[0:1, :], mod_ref[1:2, :]).astype(BF16)

        def proj(lo, hi):
            return jnp.dot(u, w_ref[:, lo:hi], preferred_element_type=F32)

        q_ref[rows, :] = (proj(0, a) * (HEAD_DIM ** -0.5)).astype(BF16)
        k_ref[rows, :] = proj(a, 2 * a).astype(BF16)
        v_ref[rows, :] = proj(2 * a, 3 * a).astype(BF16)
        o = 3 * a
        us = proj(o, o + SSM_WIDTH)
        for h in range(n_tiles):
            us_scr[h, rows, :] = us[:, h * LANES:(h + 1) * LANES]
        o += SSM_WIDTH
        sa_ref[rows, :] = jax.nn.sigmoid(proj(o, o + D_MODEL) + bg_ref[:, 0:D_MODEL]).astype(BF16)
        o += D_MODEL
        ss_ref[rows, :] = jax.nn.sigmoid(proj(o, o + D_MODEL) + bg_ref[:, D_MODEL:]).astype(BF16)
    n_rows = tm // SSM_CHUNK
    pieces = [[us_scr[h, pl.ds(s, n_rows, stride=SSM_CHUNK), :] for h in range(n_tiles)]
              for s in range(SSM_CHUNK)]
    us_ref[...] = jnp.concatenate(
        [pieces[s][h] for s in range(SSM_CHUNK) for h in range(n_tiles)], axis=1).astype(BF16)
    per_tile = LANES // SSM_GROUP_CH
    ug_ref[...] = jnp.concatenate(
        [pieces[s][g // per_tile][:, (g % per_tile) * SSM_GROUP_CH:(g % per_tile + 1) * SSM_GROUP_CH]
         for g in range(SSM_GROUPS) for s in range(SSM_CHUNK)], axis=1).astype(BF16)


def _inproj(x, mod3, g_mix, w_in, b_gate, tm):
    bsz, seq, d = x.shape
    in_width = w_in.shape[1]

    def tok(width):
        return pl.BlockSpec((None, tm, width), lambda b, s: (b, s, 0))

    def out(width):
        return jax.ShapeDtypeStruct((bsz, seq, width), BF16)

    return pl.pallas_call(
        _inproj_body,
        grid=(bsz, seq // tm),
        in_specs=[tok(d),
                  pl.BlockSpec((None, 6, d), lambda b, s: (b, 0, 0)),
                  _const_spec((1, d)),
                  _const_spec((d, in_width)),
                  _const_spec((1, 2 * d))],
        out_specs=[tok(ATT_WIDTH), tok(ATT_WIDTH), tok(ATT_WIDTH),
                   pl.BlockSpec((None, tm // SSM_CHUNK, SSM_CHUNK * SSM_WIDTH), lambda b, s: (b, s, 0)),
                   pl.BlockSpec((None, tm // SSM_CHUNK, SSM_CHUNK * SSM_WIDTH), lambda b, s: (b, s, 0)),
                   tok(d), tok(d)],
        out_shape=[out(ATT_WIDTH), out(ATT_WIDTH), out(ATT_WIDTH),
                   jax.ShapeDtypeStruct((bsz, seq // SSM_CHUNK, SSM_CHUNK * SSM_WIDTH), BF16),
                   jax.ShapeDtypeStruct((bsz, seq // SSM_CHUNK, SSM_CHUNK * SSM_WIDTH), BF16),
                   out(d), out(d)],
        scratch_shapes=[pltpu.VMEM((SSM_WIDTH // LANES, tm, LANES), F32)],
        compiler_params=_params(2),
        name="inproj",
    )(x, mod3, g_mix.reshape(1, d), w_in.astype(BF16), b_gate.reshape(1, 2 * d))


def _alibi_distance_table(dilation, kw):
    w = ATT_BLOCK
    j = np.arange(kw)[:, None]
    a = np.arange(w)[None, :]
    tabs = []
    for sel in range(2):
        dist = (sel * w + a - j).astype(np.float32)
        valid = (dist >= 0) & (dist <= w)
        tabs.append(np.where(valid, -dilation * dist, NEG_INF))
    return np.stack(tabs).astype(np.float32)


def _alibi_slope_table():
    w = ATT_BLOCK
    tabs = np.zeros((N_HEADS // 2, 2 * w, w), np.float32)
    for h in range(N_HEADS):
        slope = 2.0 ** (-8.0 * (h + 1) / N_HEADS)
        tabs[h // 2, (h % 2) * w:(h % 2 + 1) * w, :] = slope * np.eye(w, dtype=np.float32)
    return tabs


def _pair_block(qp, kp, vp, slopes, dist_t, low_half):
    w = ATT_BLOCK
    kw = kp.shape[0]
    zero = jnp.zeros((), BF16)
    qs = jnp.concatenate([jnp.where(low_half, qp, zero), jnp.where(low_half, zero, qp)], axis=0)
    lhs = jnp.concatenate([qs, slopes], axis=1)
    rhs = jnp.concatenate([kp, dist_t], axis=1)
    s2 = lax.dot_general(lhs, rhs, (((1,), (1,)), ((), ())), preferred_element_type=F32)
    v_ones = jnp.concatenate([vp, jnp.ones((kw, LANES), BF16)], axis=1)
    res, maxes = [], []
    for hh in range(2):
        s = s2[hh * w:(hh + 1) * w]
        m = jnp.max(s, axis=-1, keepdims=True)
        p = jnp.exp2(((s - m) * LOG2E).astype(BF16))
        res.append(jnp.dot(p, v_ones, preferred_element_type=F32))
        maxes.append(m)
    num = jnp.where(low_half, res[0][:, :LANES], res[1][:, :LANES])
    den = jnp.where(low_half, res[0][:, LANES:], res[1][:, LANES:])
    return num, den, jnp.where(low_half, maxes[0], maxes[1])


def _attn_body(q_ref, k_ref, v_ref, slope_ref, b1_ref, b4_ref, b16_ref, o_ref,
               f32_ref, sub4f_ref, sub4_ref, sub16_ref, num_ref, den_ref, max_ref, *, seq):
    w = ATT_BLOCK
    pair = pl.program_id(1)
    low_half = lax.broadcasted_iota(jnp.int32, (w, LANES), 1) < HEAD_DIM
    slopes = slope_ref[pair]
    srcs = (q_ref, k_ref, v_ref)
    n4, n16 = seq // 4, seq // 16
    for t in range(3):
        f32_ref[t] = srcs[t][...].astype(F32)
    for t in range(3):
        for r4 in range(4):
            sub4f_ref[t, r4 * n4:(r4 + 1) * n4, :] = f32_ref[t, pl.ds(r4, n4, stride=4), :]
    for t in range(3):
        sub4_ref[t] = sub4f_ref[t].astype(BF16)
        for r4 in range(4):
            for q4 in range(4):
                r16 = 4 * q4 + r4
                sub16_ref[t, r16 * n16:(r16 + 1) * n16, :] = (
                    sub4f_ref[t, pl.ds(r4 * n4 + q4, n16, stride=4), :].astype(BF16))

    def block(load, bias_ref, sub_len, r, i):
        base = r * sub_len
        kw = w if i == 0 else 2 * w
        krow = base + max(i - 1, 0) * w
        sel = min(i, 1)
        return _pair_block(load(0, base + i * w, w), load(1, krow, kw), load(2, krow, kw),
                           slopes, bias_ref[sel, 0:kw, :], low_half)

    dilated = ((0, 4, b4_ref, lambda t, row, size: sub4_ref[t, row:row + size, :]),
               (1, 16, b16_ref, lambda t, row, size: sub16_ref[t, row:row + size, :]))
    for pidx, dilation, bias_ref, load in dilated:
        sub_len = seq // dilation
        for r in range(dilation):
            for i in range(sub_len // w):
                num, den, m = block(load, bias_ref, sub_len, r, i)
                dst = pl.ds(dilation * w * i + r, w, stride=dilation)
                num_ref[pidx, dst, :] = num
                den_ref[pidx, dst, :] = den
                max_ref[pidx, dst, :] = m

    @pl.when(pl.program_id(0) >= 0)
    def _dense_and_merge():
        for i in range(seq // w):
            num1, den1, m1 = block(lambda t, row, size: srcs[t][row:row + size, :], b1_ref, seq, 0, i)
            rows = slice(i * w, (i + 1) * w)
            m2, m3 = max_ref[0, rows, :], max_ref[1, rows, :]
            mx = jnp.maximum(jnp.maximum(m1, m2), m3)
            e1, e2, e3 = jnp.exp(m1 - mx), jnp.exp(m2 - mx), jnp.exp(m3 - mx)
            num = e1 * num1 + e2 * num_ref[0, rows, :] + e3 * num_ref[1, rows, :]
            den = e1 * den1 + e2 * den_ref[0, rows, :] + e3 * den_ref[1, rows, :]
            o_ref[rows, :] = (num * (1.0 / den)).astype(BF16)


def _attention(q, k, v):
    bsz, seq, _ = q.shape
    w = ATT_BLOCK
    slopes = jnp.asarray(_alibi_slope_table(), BF16)
    biases = [jnp.asarray(_alibi_distance_table(d, min(2 * w, seq // d)), BF16) for _, d in PATTERNS]
    blk = pl.BlockSpec((None, seq, LANES), lambda b, p: (b, 0, p))
    return pl.pallas_call(
        functools.partial(_attn_body, seq=seq),
        grid=(bsz, N_HEADS // 2),
        in_specs=[blk, blk, blk, _const_spec(slopes.shape)] + [_const_spec(t.shape) for t in biases],
        out_specs=blk,
        out_shape=jax.ShapeDtypeStruct((bsz, seq, ATT_WIDTH), BF16),
        scratch_shapes=[pltpu.VMEM((3, seq, LANES), F32),
                        pltpu.VMEM((3, seq, LANES), F32),
                        pltpu.VMEM((3, seq, LANES), BF16),
                        pltpu.VMEM((3, seq, LANES), BF16),
                        pltpu.VMEM((2, seq, LANES), F32),
                        pltpu.VMEM((2, seq, LANES), F32),
                        pltpu.VMEM((2, seq, LANES), F32)],
        compiler_params=_params(2),
        name="attn",
    )(q, k, v, slopes, *biases)


def _placement(n_outer, n_inner):
    t = np.zeros((n_outer, n_inner, n_outer, SSM_GROUPS, n_inner), np.float32)
    for o in range(n_outer):
        for i in range(n_inner):
            t[o, i, o, :, i] = 1.0
    return t.reshape(n_outer * n_inner, n_outer * SSM_GROUPS * n_inner)


def _block_diagonal(val, n_outer, n_inner, row_inner):
    full = jnp.dot(val.astype(BF16), jnp.asarray(_placement(n_outer, n_inner)).astype(BF16),
                   preferred_element_type=F32)
    row_g = (lax.broadcasted_iota(jnp.int32, full.shape, 0) // row_inner) % SSM_GROUPS
    col_g = (lax.broadcasted_iota(jnp.int32, full.shape, 1) // n_inner) % SSM_GROUPS
    return jnp.where(row_g == col_g, full, 0.0).astype(BF16)


def _ssm_matrices(a_re, a_im, log_dt, b_re, b_im, c_re, c_im, d_skip):
    lc, g_n, n_st, ch = SSM_CHUNK, SSM_GROUPS, SSM_STATE, SSM_GROUP_CH
    hi = lax.Precision.HIGHEST
    lr, li = a_re.astype(F32), a_im.astype(F32)
    dt = jnp.exp(log_dt.astype(F32))[:, None]
    mag = jnp.exp(lr * dt)
    ang = li * dt
    ab_re, ab_im = mag * jnp.cos(ang), mag * jnp.sin(ang)
    nr, ni = ab_re - 1.0, ab_im
    den = lr * lr + li * li
    f_re = (nr * lr + ni * li) / den
    f_im = (ni * lr - nr * li) / den
    bb_re = f_re[..., None] * b_re - f_im[..., None] * b_im
    bb_im = f_re[..., None] * b_im + f_im[..., None] * b_re

    kk = jnp.arange(lc + 1, dtype=F32)[:, None, None]
    pw_mag = jnp.exp(kk * (lr * dt)[None])
    pw_re = pw_mag * jnp.cos(kk * ang[None])
    pw_im = pw_mag * jnp.sin(kk * ang[None])

    cp_re = c_re[None] * pw_re[:, :, None, :] - c_im[None] * pw_im[:, :, None, :]
    cp_im = c_re[None] * pw_im[:, :, None, :] + c_im[None] * pw_re[:, :, None, :]

    taps = (jnp.einsum('kgcn,gni->kgci', cp_re[:lc], bb_re, precision=hi)
            - jnp.einsum('kgcn,gni->kgci', cp_im[:lc], bb_im, precision=hi))
    skip = d_skip.reshape(g_n, ch)
    taps = taps.at[0].add(skip[:, :, None] * jnp.eye(ch, dtype=F32)[None])
    lag = np.arange(lc)[None, :] - np.arange(lc)[:, None]
    tz = taps[np.maximum(lag, 0)] * jnp.asarray(lag >= 0, F32)[:, :, None, None, None]
    tz = tz.transpose(0, 2, 4, 1, 3).reshape(lc * SSM_WIDTH, lc * ch)
    toeplitz = _block_diagonal(tz, lc, ch, ch)

    cv = jnp.stack([cp_re[1:], -cp_im[1:]], axis=0)
    cv = cv.transpose(0, 2, 4, 1, 3).reshape(STATE_COLS, lc * ch)
    carry = _block_diagonal(cv, lc, ch, n_st)

    rp_re, rp_im = pw_re[:lc][::-1], pw_im[:lc][::-1]
    bp_re = rp_re[..., None] * bb_re[None] - rp_im[..., None] * bb_im[None]
    bp_im = rp_re[..., None] * bb_im[None] + rp_im[..., None] * bb_re[None]
    bv = jnp.stack([bp_re, bp_im], axis=0).reshape(2, lc, g_n // 2, 2, n_st, ch)
    bst = jnp.einsum('pskinc,ij->kiscpjn', bv, jnp.eye(2, dtype=F32))
    bst = bst.reshape(g_n // 2, 2 * lc * ch, 4 * n_st).astype(BF16)

    half = g_n * n_st
    a_pow = jnp.concatenate([pw_re[lc].reshape(1, half), pw_im[lc].reshape(1, half)], axis=1)
    return bst, carry, toeplitz, a_pow


def _ssm_body(u_ref, ug_ref, bst_ref, wc_ref, wt_ref, apow_ref, wglu_ref, bglu_ref, o_ref,
              st_ref, y_ref, tok_ref, *, nb, n_chunks):
    lc = SSM_CHUNK
    half = STATE_COLS // 2
    rows = nb * n_chunks
    u2 = u_ref[...].reshape(rows, lc * SSM_WIDTH)
    ug = ug_ref[...].reshape(rows, lc * SSM_WIDTH)
    pw = 2 * SSM_STATE
    for k in range(SSM_GROUPS // 2):
        res = jnp.dot(ug[:, 2 * k * pw:2 * (k + 1) * pw], bst_ref[k], preferred_element_type=F32)
        st_ref[:, k * pw:(k + 1) * pw] = res[:, :pw]
        st_ref[:, half + k * pw:half + (k + 1) * pw] = res[:, pw:]

    a_r = apow_ref[:, 0:half]
    a_i = apow_ref[:, half:]

    hw = lc * SSM_WIDTH // 2
    for half_idx in range(2):
        cols = slice(half_idx * hw, (half_idx + 1) * hw)
        kdim = (half_idx + 1) * hw
        y_ref[:, cols] = jnp.dot(u2[:, :kdim], wt_ref[0:kdim, cols], preferred_element_type=F32)

    state = [jnp.zeros((1, half), F32) for _ in range(2 * nb)]
    for c in range(n_chunks):
        for b in range(nb):
            x_r, x_i = state[2 * b], state[2 * b + 1]
            row = slice(b * n_chunks + c, b * n_chunks + c + 1)
            s_r = st_ref[row, 0:half]
            s_i = st_ref[row, half:]
            st_ref[row, 0:half] = x_r
            st_ref[row, half:] = x_i
            state[2 * b] = a_r * x_r - a_i * x_i + s_r
            state[2 * b + 1] = a_r * x_i + a_i * x_r + s_i

    x_in = st_ref[...].astype(BF16)
    for half_idx in range(2):
        cols = slice(half_idx * hw, (half_idx + 1) * hw)
        y_ref[:, cols] += jnp.dot(x_in, wc_ref[:, cols], preferred_element_type=F32)
    for j in range(lc):
        y = jax.nn.gelu(y_ref[:, j * SSM_WIDTH:(j + 1) * SSM_WIDTH])
        gate = jnp.dot(y.astype(BF16), wglu_ref[...], preferred_element_type=F32) + bglu_ref[...]
        res = y * jax.nn.sigmoid(gate)
        for h in range(SSM_WIDTH // LANES):
            tok_ref[h, pl.ds(j, rows, stride=lc), :] = res[:, h * LANES:(h + 1) * LANES]
    out = jnp.concatenate([tok_ref[h] for h in range(SSM_WIDTH // LANES)], axis=1)
    o_ref[...] = out.astype(BF16).reshape(nb, n_chunks * lc, SSM_WIDTH)


def _ssm(u2, ug, bst, carry, toeplitz, a_pow, w_glu, b_glu, nb):
    bsz, n_chunks, _ = u2.shape
    lc = SSM_CHUNK
    seq = n_chunks * lc
    return pl.pallas_call(
        functools.partial(_ssm_body, nb=nb, n_chunks=n_chunks),
        grid=(bsz // nb,),
        in_specs=[pl.BlockSpec((nb, n_chunks, lc * SSM_WIDTH), lambda b: (b, 0, 0)),
                  pl.BlockSpec((nb, n_chunks, lc * SSM_WIDTH), lambda b: (b, 0, 0)),
                  _const_spec(bst.shape), _const_spec(carry.shape), _const_spec(toeplitz.shape),
                  _const_spec(a_pow.shape),
                  _const_spec((SSM_WIDTH, SSM_WIDTH)), _const_spec((1, SSM_WIDTH))],
        out_specs=pl.BlockSpec((nb, seq, SSM_WIDTH), lambda b: (b, 0, 0)),
        out_shape=jax.ShapeDtypeStruct((bsz, seq, SSM_WIDTH), BF16),
        scratch_shapes=[pltpu.VMEM((nb * n_chunks, STATE_COLS), F32),
                        pltpu.VMEM((nb * n_chunks, lc * SSM_WIDTH), F32),
                        pltpu.VMEM((SSM_WIDTH // LANES, nb * seq, LANES), F32)],
        compiler_params=_params(1),
        name="ssm",
    )(u2, ug, bst, carry, toeplitz, a_pow, w_glu.astype(BF16), b_glu.reshape(1, SSM_WIDTH))


def _tail_body(x_ref, oatt_ref, sg_ref, sa_ref, ss_ref,
               mod_ref, wpa_ref, wps_ref, wo_ref, gf_ref, wup_ref, wc_ref, bc_ref,
               wdn_ref, gfin_ref, out_ref, carry_ref):
    tm = x_ref.shape[0]

    @pl.when(pl.program_id(1) == 0)
    def _():
        carry_ref[...] = jnp.zeros_like(carry_ref)

    hm = tm // TAIL_MIX_ROW_SLICES
    h1_parts = []
    for part in range(TAIL_MIX_ROW_SLICES):
        rows = slice(part * hm, (part + 1) * hm)
        y_att = jnp.dot(oatt_ref[rows, :], wpa_ref[...], preferred_element_type=F32)
        y_ssm = jnp.dot(sg_ref[rows, :], wps_ref[...], preferred_element_type=F32)
        merged = sa_ref[rows, :].astype(F32) * y_att + ss_ref[rows, :].astype(F32) * y_ssm
        mix = jnp.dot(merged.astype(BF16), wo_ref[...], preferred_element_type=F32)
        h1_parts.append(x_ref[rows, :] + mod_ref[2:3, :] * mix)
    h1 = jnp.concatenate(h1_parts, axis=0)

    u = _rms_modulate(h1, gf_ref[...], mod_ref[3:4, :], mod_ref[4:5, :]).astype(BF16)
    a = jnp.dot(u, wup_ref[:, 0:D_FF], preferred_element_type=F32)
    val = jnp.dot(u, wup_ref[:, D_FF:], preferred_element_type=F32)

    row = lax.broadcasted_iota(jnp.int32, a.shape, 0)
    prev1 = carry_ref[SUBLANES - 1:SUBLANES, :]
    prev2 = carry_ref[SUBLANES - 2:SUBLANES - 1, :]
    a1 = jnp.where(row == 0, prev1, pltpu.roll(a, 1, 0))
    a2 = jnp.where(row == 0, prev2, jnp.where(row == 1, prev1, pltpu.roll(a, 2, 0)))
    carry_ref[...] = a[tm - SUBLANES:, :]
    conv = bc_ref[...] + wc_ref[0:1, :] * a + wc_ref[1:2, :] * a1 + wc_ref[2:3, :] * a2
    act = (conv * jax.nn.sigmoid(conv) * val).astype(BF16)
    for part in range(TAIL_MIX_ROW_SLICES):
        rows = slice(part * hm, (part + 1) * hm)
        ffn = jnp.dot(act[rows], wdn_ref[...], preferred_element_type=F32)
        h2 = h1_parts[part] + mod_ref[5:6, :] * ffn
        ms = jnp.mean(h2 * h2, axis=-1, keepdims=True)
        out_ref[rows, :] = (h2 * lax.rsqrt(ms + EPS)) * gfin_ref[...]


def _tail(x, o_att, sglu, sig_att, sig_ssm, mod3, w_proj_att, w_proj_ssm, w_out,
          g_ffn, w_up, w_conv, b_conv, w_down, g_final, tm):
    bsz, seq, d = x.shape

    def tok(width):
        return pl.BlockSpec((None, tm, width), lambda b, s: (b, s, 0))

    in_specs = [tok(d), tok(ATT_WIDTH), tok(SSM_WIDTH), tok(d), tok(d),
                pl.BlockSpec((None, 6, d), lambda b, s: (b, 0, 0)),
                _const_spec((ATT_WIDTH, d)), _const_spec((SSM_WIDTH, d)), _const_spec((d, d)),
                _const_spec((1, d)), _const_spec((d, 2 * D_FF)), _const_spec((CONV_W, D_FF)),
                _const_spec((1, D_FF)), _const_spec((D_FF, d)), _const_spec((1, d))]
    return pl.pallas_call(
        _tail_body,
        grid=(bsz, seq // tm),
        in_specs=in_specs,
        out_specs=tok(d),
        out_shape=jax.ShapeDtypeStruct((bsz, seq, d), F32),
        scratch_shapes=[pltpu.VMEM((SUBLANES, D_FF), F32)],
        compiler_params=_params(2, TAIL_VMEM_LIMIT_BYTES),
        name="tail",
    )(x, o_att, sglu, sig_att, sig_ssm, mod3,
      w_proj_att.astype(BF16), w_proj_ssm.astype(BF16), w_out.astype(BF16),
      g_ffn.reshape(1, d), w_up.astype(BF16), w_conv, b_conv.reshape(1, D_FF),
      w_down.astype(BF16), g_final.reshape(1, d))


def kernel(x, c, w_ada, b_ada, g_mix, w_in, b_gate, a_re, a_im, log_dt, b_re, b_im, c_re, c_im,
           d_skip, w_glu, b_glu, w_proj_att, w_proj_ssm, w_out, g_ffn, w_up, w_conv, b_conv,
           w_down, g_final):
    depth = w_ada.shape[0]
    assert depth == 1, "the final RMSNorm is fused into the single layer's tail kernel"
    bsz, seq, d = x.shape
    l = 0
    mod3 = _ada(c, w_ada[l], b_ada[l]).reshape(bsz, 6, d)
    q, k, v, us, ug, sig_att, sig_ssm = _inproj(x, mod3, g_mix[l], w_in[l], b_gate[l], tm=1024)

    o_att = _attention(q, k, v)

    ssm_mats = _ssm_matrices(a_re[l], a_im[l], log_dt[l], b_re[l], b_im[l], c_re[l], c_im[l],
                             d_skip[l])
    sglu = _ssm(us, ug, *ssm_mats, w_glu[l], b_glu[l], nb=2)

    return _tail(x, o_att, sglu, sig_att, sig_ssm, mod3, w_proj_att[l], w_proj_ssm[l],
                 w_out[l], g_ffn[l], w_up[l], w_conv[l], b_conv[l], w_down[l], g_final, tm=1024)
```
